```python
import jax, jax.numpy as jnp
from jax import lax
import numpy as np

D_MODEL = 1024
BATCH = 2
SEQ = 8192
DEPTH = 2
DEC_BATCH = 32
DEC_SEQ = 16
PAST_LEN = 2048

CHUNK = 64
HEAD_DIM = 64
MIX_HEADS = D_MODEL // HEAD_DIM
A_GROUPS = MIX_HEADS // 4
C_HEADS = (MIX_HEADS - A_GROUPS) // 2
B_HEADS = MIX_HEADS - A_GROUPS - C_HEADS
B_KV_HEADS = 1
A_WIDTH = A_GROUPS * HEAD_DIM
B_WIDTH = B_HEADS * HEAD_DIM
B_KV_WIDTH = B_KV_HEADS * HEAD_DIM
C_WIDTH = C_HEADS * HEAD_DIM
D_MIX = A_WIDTH + B_WIDTH + C_WIDTH
A_CHUNK = 128
IDX_HEADS = 8
IDX_DIM = 32
TOPK_MAX = 256
Q_BLOCK = 128
ROPE_BASE = 10000.0
D_FF = 2816
CONV_W = 3
LN_EPS = 1e-5
ALPHA = (2 * DEPTH) ** 0.25
BETA = (8 * DEPTH) ** -0.25
PROJ_SIZES = (A_WIDTH, A_WIDTH, B_WIDTH, B_KV_WIDTH, B_KV_WIDTH, IDX_HEADS * IDX_DIM, IDX_DIM,
              IDX_HEADS, C_WIDTH, C_WIDTH, C_WIDTH, C_WIDTH)
PROJ_SPLITS = tuple(int(s) for s in np.cumsum(PROJ_SIZES)[:-1])
D_PROJ = int(sum(PROJ_SIZES))

kernel_name = 'hybrid_streaming_encoder_step'

F32 = jnp.float32


def layer_norm(x, g, b):
    xf = x.astype(F32)
    mu = xf.mean(-1, keepdims=True)
    var = jnp.square(xf - mu).mean(-1, keepdims=True)
    return ((xf - mu) * lax.rsqrt(var + LN_EPS) * g + b).astype(x.dtype)


def head_norm(y, g):
    yf = y.astype(F32)
    mu = yf.mean(-1, keepdims=True)
    var = jnp.square(yf - mu).mean(-1, keepdims=True)
    return (yf - mu) * lax.rsqrt(var + LN_EPS) * g.reshape(C_HEADS, HEAD_DIM)


def rotary(x, pos):
    half = HEAD_DIM // 2
    freqs = ROPE_BASE ** (-jnp.arange(half, dtype=F32) / half)
    ang = pos.astype(F32)[:, None] * freqs
    cos, sin = jnp.cos(ang)[None, :, None, :], jnp.sin(ang)[None, :, None, :]
    xf = x.astype(F32)
    x1, x2 = xf[..., :half], xf[..., half:]
    return jnp.concatenate([x1 * cos - x2 * sin, x1 * sin + x2 * cos], -1).astype(x.dtype)


def mixer_a(u_raw, v_raw, ln_g, ln_b, ws, bs, chunk_len):
    u = jax.nn.gelu(u_raw)
    v = jax.nn.gelu(v_raw)
    bn, t, _ = v.shape
    n = t // chunk_len
    vn = layer_norm(v.reshape(bn, t, A_GROUPS, HEAD_DIM), ln_g.reshape(A_GROUPS, HEAD_DIM),
                    ln_b.reshape(A_GROUPS, HEAD_DIM))
    vc = vn.reshape(bn, n, chunk_len, A_GROUPS, HEAD_DIM)
    w = ws[:, :chunk_len, :chunk_len] * jnp.tril(jnp.ones((chunk_len, chunk_len), ws.dtype))
    s = jnp.einsum('gij,bnjgc->bnigc', w, vc) + bs[:, :chunk_len].T[None, None, :, :, None]
    return u * s.reshape(bn, t, A_WIDTH), vn.reshape(bn, t, A_WIDTH)


def dsa_attend(q, qi, wi, k, v, ki, limit, k_sel):
    bn, t, _ = q.shape
    l = k.shape[1]
    qi = qi.reshape(bn, t, IDX_HEADS, IDX_DIM)
    rel = jax.nn.relu(jnp.einsum('bthd,bsd->bths', qi, ki).astype(F32) * IDX_DIM ** -0.5)
    score = jnp.einsum('bths,bth->bts', rel, wi.astype(F32)) * IDX_HEADS ** -0.5
    adm = jnp.arange(l)[None, :] < limit[:, None]
    score = jnp.where(adm[None], score, -jnp.inf)
    _, idx = lax.top_k(score, k_sel)
    valid = idx < limit[None, :, None]
    gather = jax.vmap(lambda a, i: a[i])
    ks = gather(k, idx).reshape(bn, t, k_sel, B_KV_HEADS, HEAD_DIM)
    vs = gather(v, idx).reshape(bn, t, k_sel, B_KV_HEADS, HEAD_DIM)
    qh = q.reshape(bn, t, B_KV_HEADS, B_HEADS // B_KV_HEADS, HEAD_DIM)
    logits = jnp.einsum('btngd,btknd->btngk', qh, ks).astype(F32) * HEAD_DIM ** -0.5
    logits = jnp.where(valid[:, :, None, None, :], logits, -jnp.inf)
    p = jax.nn.softmax(logits, axis=-1).astype(v.dtype)
    o = jnp.einsum('btngk,btknd->btngd', p, vs)
    return o.reshape(bn, t, B_WIDTH)


def dsa_prompt(q, qi, wi, k, v, ki):
    bn, s, _ = q.shape
    nb = s // Q_BLOCK
    k_sel = min(TOPK_MAX, s // 4)

    def blocks(a):
        return a.reshape(bn, nb, Q_BLOCK, a.shape[-1]).swapaxes(0, 1)

    pos = jnp.arange(s).reshape(nb, Q_BLOCK)
    limit = (pos // CHUNK + 1) * CHUNK

    def one(args):
        qb, qib, wib, lim = args
        return dsa_attend(qb, qib, wib, k, v, ki, lim, k_sel)

    o = lax.map(one, (blocks(q), blocks(qi), blocks(wi), limit))
    return o.swapaxes(0, 1).reshape(bn, s, B_WIDTH)


def retention_terms(q, k, v, log_g):
    c = q.shape[2]
    i = jnp.arange(c, dtype=F32)
    diff = i[:, None] - i[None, :]
    decay = jnp.where(diff >= 0, jnp.exp(jnp.maximum(diff, 0.0)[None] * log_g[:, None, None]), 0.0)
    s = jnp.einsum('bnihd,bnjhd->bnhij', q, k) * decay
    intra = jnp.einsum('bnhij,bnjhe->bnihe', s, v)
    kv = jnp.einsum('bnjhd,hj,bnjhe->bnhde', k, jnp.exp((c - 1 - i)[None] * log_g[:, None]), v)
    q_decay = jnp.exp((i + 1)[None] * log_g[:, None])
    chunk_decay = jnp.exp(c * log_g)
    return intra, kv, q_decay, chunk_decay


def mixer_c(qr, kr, vr, gr, gn_g, pos, r0, chunk_len):
    bn, t, _ = qr.shape
    n = t // chunk_len
    log_g = jnp.log(1.0 - 2.0 ** (-5.0 - jnp.arange(C_HEADS, dtype=F32)))
    q = rotary(qr.reshape(bn, t, C_HEADS, HEAD_DIM), pos)
    k = rotary(kr.reshape(bn, t, C_HEADS, HEAD_DIM), pos) * HEAD_DIM ** -0.5
    sh = (bn, n, chunk_len, C_HEADS, HEAD_DIM)
    q = q.reshape(sh)
    intra, kv, q_decay, chunk_decay = retention_terms(q, k.reshape(sh), vr.reshape(sh), log_g)

    def step(r, kv_n):
        return chunk_decay[None, :, None, None] * r + kv_n, r

    r_final, r_prev = lax.scan(step, r0.astype(F32), kv.swapaxes(0, 1).astype(F32))
    cross = jnp.einsum('bnihd,nbhde,hi->bnihe', q, r_prev, q_decay)
    y = head_norm((intra + cross).reshape(bn, t, C_HEADS, HEAD_DIM), gn_g)
    return jax.nn.silu(gr) * y.reshape(bn, t, C_WIDTH), r_final


def conv_ffn(x, conv_prev, w_gate, w_up, conv_w, conv_b, w_down):
    t = x.shape[1]
    hg = x @ w_gate
    hu = x @ w_up
    ext = jnp.concatenate([conv_prev.astype(hg.dtype), hg], axis=1)
    conv = conv_b + sum(conv_w[j] * ext[:, j:j + t] for j in range(CONV_W))
    return (jax.nn.gelu(conv) * hu) @ w_down, ext[:, t:]


def finish(x, mix, w_out, ln1_g, ln1_b, conv_prev, w_gate, w_up, conv_w, conv_b, w_down, ln2_g, ln2_b):
    x = layer_norm(ALPHA * x + (mix @ w_out).astype(x.dtype), ln1_g, ln1_b)
    f, conv_state = conv_ffn(x, conv_prev, w_gate, w_up, conv_w, conv_b, w_down)
    x = layer_norm(ALPHA * x + f.astype(x.dtype), ln2_g, ln2_b)
    return x, conv_state


def setup_inputs(seed: int = 0) -> dict:
    key = jax.random.key(seed)
    ks = jax.random.split(key, 24)

    def nrm(k, shape, scale):
        return jax.random.normal(k, shape, F32) * scale

    return {
        'x_prompt': nrm(ks[0], (BATCH, SEQ, D_MODEL), 1.0),
        'x_sample': nrm(ks[1], (DEC_BATCH, DEC_SEQ, D_MODEL), 1.0),
        'cache_b_k': nrm(ks[2], (DEPTH, DEC_BATCH, PAST_LEN, B_KV_WIDTH), 1.0),
        'cache_b_v': nrm(ks[3], (DEPTH, DEC_BATCH, PAST_LEN, B_KV_WIDTH), 1.0),
        'cache_b_kidx': nrm(ks[4], (DEPTH, DEC_BATCH, PAST_LEN, IDX_DIM), 1.0),
        'state_ret': nrm(ks[5], (DEPTH, DEC_BATCH, C_HEADS, HEAD_DIM, HEAD_DIM), 0.5),
        'state_ffn_conv': nrm(ks[6], (DEPTH, DEC_BATCH, CONV_W - 1, D_FF), 1.0),
        'w_in': nrm(ks[7], (DEPTH, D_MODEL, D_PROJ), D_MODEL ** -0.5),
        'a_ln_g': 1.0 + nrm(ks[8], (DEPTH, A_WIDTH), 0.01),
        'a_ln_b': nrm(ks[9], (DEPTH, A_WIDTH), 0.01),
        'a_ws': nrm(ks[10], (DEPTH, A_GROUPS, A_CHUNK, A_CHUNK), A_CHUNK ** -0.5),
        'a_bs': 1.0 + nrm(ks[11], (DEPTH, A_GROUPS, A_CHUNK), 0.01),
        'c_gn_g': 1.0 + nrm(ks[12], (DEPTH, C_WIDTH), 0.01),
        'w_out': nrm(ks[13], (DEPTH, D_MIX, D_MODEL), BETA * D_MIX ** -0.5),
        'ln1_g': 1.0 + nrm(ks[14], (DEPTH, D_MODEL), 0.01),
        'ln1_b': nrm(ks[15], (DEPTH, D_MODEL), 0.01),
        'w_gate': nrm(ks[16], (DEPTH, D_MODEL, D_FF), D_MODEL ** -0.5),
        'w_up': nrm(ks[17], (DEPTH, D_MODEL, D_FF), D_MODEL ** -0.5),
        'conv_w': nrm(ks[18], (DEPTH, CONV_W, D_FF), CONV_W ** -0.5),
        'conv_b': nrm(ks[19], (DEPTH, D_FF), 0.01),
        'w_down': nrm(ks[20], (DEPTH, D_FF, D_MODEL), BETA * D_FF ** -0.5),
        'ln2_g': 1.0 + nrm(ks[21], (DEPTH, D_MODEL), 0.01),
        'ln2_b': nrm(ks[22], (DEPTH, D_MODEL), 0.01),
    }


def reference(x_prompt, x_sample, cache_b_k, cache_b_v, cache_b_kidx, state_ret, state_ffn_conv,
              w_in, a_ln_g, a_ln_b, a_ws, a_bs, c_gn_g, w_out, ln1_g, ln1_b,
              w_gate, w_up, conv_w, conv_b, w_down, ln2_g, ln2_b):
    bp, s, _ = x_prompt.shape
    bs_, t, _ = x_sample.shape
    past = cache_b_k.shape[2]
    l_keys = past + t
    k_sel_s = min(TOPK_MAX, l_keys // 4)
    pos_p = jnp.arange(s)
    pos_s = past + jnp.arange(t)
    xp, xs = x_prompt, x_sample
    kp, vp, kip, rp, cp = [], [], [], [], []
    ks_, vs_, kis, rs, cs, avs = [], [], [], [], [], []
    for l in range(DEPTH):
        ua, va, qb, kb, vb, qib, kib, wib, qc, kc, vc, gc = jnp.split(xp @ w_in[l], PROJ_SPLITS, axis=-1)
        oa, _ = mixer_a(ua, va, a_ln_g[l], a_ln_b[l], a_ws[l], a_bs[l], A_CHUNK)
        ob = dsa_prompt(qb, qib, wib, kb, vb, kib)
        oc, r_p = mixer_c(qc, kc, vc, gc, c_gn_g[l], pos_p,
                          jnp.zeros((bp, C_HEADS, HEAD_DIM, HEAD_DIM), F32), CHUNK)
        mix = jnp.concatenate([oa, ob.astype(oa.dtype), oc.astype(oa.dtype)], axis=-1)
        xp, c_p = finish(xp, mix, w_out[l], ln1_g[l], ln1_b[l],
                         jnp.zeros((bp, CONV_W - 1, D_FF), xp.dtype),
                         w_gate[l], w_up[l], conv_w[l], conv_b[l], w_down[l], ln2_g[l], ln2_b[l])
        kp.append(kb); vp.append(vb); kip.append(kib); rp.append(r_p); cp.append(c_p)
        ua, va, qb, kb, vb, qib, kib, wib, qc, kc, vc, gc = jnp.split(xs @ w_in[l], PROJ_SPLITS, axis=-1)
        oa, av = mixer_a(ua, va, a_ln_g[l], a_ln_b[l], a_ws[l], a_bs[l], t)
        k_full = jnp.concatenate([cache_b_k[l].astype(kb.dtype), kb], axis=1)
        v_full = jnp.concatenate([cache_b_v[l].astype(vb.dtype), vb], axis=1)
        ki_full = jnp.concatenate([cache_b_kidx[l].astype(kib.dtype), kib], axis=1)
        ob = dsa_attend(qb, qib, wib, k_full, v_full, ki_full,
                        jnp.full((t,), l_keys, jnp.int32), k_sel_s)
        oc, r_s = mixer_c(qc, kc, vc, gc, c_gn_g[l], pos_s, state_ret[l], t)
        mix = jnp.concatenate([oa, ob.astype(oa.dtype), oc.astype(oa.dtype)], axis=-1)
        xs, c_s = finish(xs, mix, w_out[l], ln1_g[l], ln1_b[l], state_ffn_conv[l],
                         w_gate[l], w_up[l], conv_w[l], conv_b[l], w_down[l], ln2_g[l], ln2_b[l])
        ks_.append(kb); vs_.append(vb); kis.append(kib); rs.append(r_s); cs.append(c_s); avs.append(av)
    return (xp, xs, jnp.stack(kp), jnp.stack(vp), jnp.stack(kip), jnp.stack(rp), jnp.stack(cp),
            jnp.stack(ks_), jnp.stack(vs_), jnp.stack(kis), jnp.stack(rs), jnp.stack(cs), jnp.stack(avs))
```

```python
import functools

import numpy as np
import jax
import jax.numpy as jnp
from jax import lax
from jax.experimental import pallas as pl
from jax.experimental.pallas import tpu as pltpu

F32 = jnp.float32
BF16 = jnp.bfloat16
I32 = jnp.int32

D_MODEL = 1024
CHUNK = 64
HEAD_DIM = 64
A_GROUPS = 4
C_HEADS = 6
B_HEADS = 6
A_WIDTH = A_GROUPS * HEAD_DIM
B_WIDTH = B_HEADS * HEAD_DIM
C_WIDTH = C_HEADS * HEAD_DIM
A_CHUNK = 128
IDX_HEADS = 8
IDX_DIM = 32
TOPK_MAX = 256
ROPE_BASE = 10000.0
D_FF = 2816
CONV_W = 3
LN_EPS = 1e-5

LANE = 128
KEY_BLOCK = 128
VMEM_LIMIT = 48 * 1024 * 1024

SEG_A = 2 * A_WIDTH
SEG_QB = B_HEADS * LANE
SEG_RET = 4 * C_WIDTH
SEG_QI = IDX_HEADS * LANE
SEG_KV = LANE
SEG_KW = LANE
NEG_INF_KEY = np.int32(np.uint32(0x807FFFFF).astype(np.int64) - (1 << 32))
INT_MIN = np.int32(-(2 ** 31))


def _dot(a, b):
    return jnp.dot(a, b, preferred_element_type=F32)


def _dot_nt(a, b):
    return lax.dot_general(a, b, (((1,), (1,)), ((), ())), preferred_element_type=F32)


def _dot_tn(a, b):
    return lax.dot_general(a, b, (((0,), (0,)), ((), ())), preferred_element_type=F32)


def _split_dot(x, m):
    hi = x.astype(BF16)
    lo = (x - hi.astype(F32)).astype(BF16)
    return _dot(hi, m) + _dot(lo, m)


def _group_norm(x, gmat):
    mu = _split_dot(x, gmat)
    d = x - mu
    var = _split_dot(d * d, gmat)
    return d * lax.rsqrt(var + LN_EPS)


def _layer_norm(x, g, b):
    mu = jnp.mean(x, axis=-1, keepdims=True)
    d = x - mu
    var = jnp.mean(d * d, axis=-1, keepdims=True)
    return d * lax.rsqrt(var + LN_EPS) * g + b


def _params(n_axes):
    return pltpu.CompilerParams(dimension_semantics=("arbitrary",) * n_axes,
                                vmem_limit_bytes=VMEM_LIMIT)


def _proj_kernel(x_ref, w_ref, a_ref, qb_ref, ret_ref, qi_ref, kv_ref, kw_ref, kvb_ref, kwb_ref):
    xb = x_ref[...].astype(BF16)
    o = 0
    a_ref[...] = _dot(xb, w_ref[:, o:o + SEG_A]); o += SEG_A
    qb_ref[...] = _dot(xb, w_ref[:, o:o + SEG_QB]).astype(BF16); o += SEG_QB
    ret_ref[...] = _dot(xb, w_ref[:, o:o + SEG_RET]); o += SEG_RET
    qi_ref[...] = _dot(xb, w_ref[:, o:o + SEG_QI]).astype(BF16); o += SEG_QI
    kv = _dot(xb, w_ref[:, o:o + SEG_KV]); o += SEG_KV
    kw = _dot(xb, w_ref[:, o:o + SEG_KW])
    kv_ref[...] = kv
    kw_ref[...] = kw
    kvb_ref[...] = kv.astype(BF16)
    kwb_ref[...] = kw.astype(BF16)


def _proj(x2d, w_proj, tm):
    t = x2d.shape[0]
    n_cols = w_proj.shape[1]
    widths = (SEG_A, SEG_QB, SEG_RET, SEG_QI, SEG_KV, SEG_KW, SEG_KV, SEG_KW)
    dtypes = (F32, BF16, F32, BF16, F32, F32, BF16, BF16)
    return pl.pallas_call(
        _proj_kernel,
        grid=(t // tm,),
        in_specs=[pl.BlockSpec((tm, D_MODEL), lambda i: (i, 0)),
                  pl.BlockSpec((D_MODEL, n_cols), lambda i: (0, 0))],
        out_specs=[pl.BlockSpec((tm, w), lambda i: (i, 0)) for w in widths],
        out_shape=[jax.ShapeDtypeStruct((t, w), d) for w, d in zip(widths, dtypes)],
        compiler_params=_params(1),
        name="proj",
    )(x2d, w_proj)


def _mixa_kernel(pa_ref, wcat_ref, bias_ref, g_ref, b_ref, gmat_ref, oa_ref, av_ref, *, chunk, rows):
    pa = pa_ref[...]
    u = jax.nn.gelu(pa[:, :A_WIDTH])
    v = jax.nn.gelu(pa[:, A_WIDTH:])
    vn = _group_norm(v, gmat_ref[...]) * g_ref[...] + b_ref[...]
    av_ref[...] = vn
    group = lax.broadcasted_iota(I32, (chunk, A_WIDTH), 1) // HEAD_DIM
    wcat = wcat_ref[...]
    bias = bias_ref[...]
    for c in range(rows // chunk):
        vc = vn[c * chunk:(c + 1) * chunk, :]
        stacked = jnp.concatenate(
            [jnp.where(group == g, vc, 0.0).astype(BF16) for g in range(A_GROUPS)], axis=0)
        s = _dot(wcat, stacked) + bias
        oa_ref[c * chunk:(c + 1) * chunk, :] = u[c * chunk:(c + 1) * chunk, :] * s


def _mixa(pa, wcat, bias, ln_g, ln_b, gmat, chunk, rows):
    t = pa.shape[0]
    const = lambda i: (0, 0)
    return pl.pallas_call(
        functools.partial(_mixa_kernel, chunk=chunk, rows=rows),
        grid=(t // rows,),
        in_specs=[pl.BlockSpec((rows, SEG_A), lambda i: (i, 0)),
                  pl.BlockSpec(wcat.shape, const),
                  pl.BlockSpec(bias.shape, const),
                  pl.BlockSpec((1, A_WIDTH), const),
                  pl.BlockSpec((1, A_WIDTH), const),
                  pl.BlockSpec(gmat.shape, const)],
        out_specs=[pl.BlockSpec((rows, A_WIDTH), lambda i: (i, 0)),
                   pl.BlockSpec((rows, A_WIDTH), lambda i: (i, 0))],
        out_shape=[jax.ShapeDtypeStruct((t, A_WIDTH), F32),
                   jax.ShapeDtypeStruct((t, A_WIDTH), F32)],
        compiler_params=_params(1),
        name="mixa",
    )(pa, wcat, bias, ln_g, ln_b, gmat)


def _swap_halves(x):
    n = x.shape[1]
    lane = lax.broadcasted_iota(I32, x.shape, 1)
    fwd = pltpu.roll(x, n - HEAD_DIM // 2, axis=1)
    bwd = pltpu.roll(x, HEAD_DIM // 2, axis=1)
    return jnp.where(lane % HEAD_DIM < HEAD_DIM // 2, fwd, bwd)


def _ret_kernel(pr_ref, cos_ref, sin_ref, dec_ref, qd_ref, kd_ref, cd_ref, bd_ref, gng_ref, gmat_ref,
                r0_ref, oc_ref, rout_ref, state_ref, *, chunk, rows):
    @pl.when(pl.program_id(1) == 0)
    def _():
        state_ref[...] = r0_ref[0]

    lane = lax.broadcasted_iota(I32, (chunk, LANE), 1)
    first = (lane < HEAD_DIM).astype(F32)
    second = 1.0 - first
    qd = qd_ref[...]
    kd = kd_ref[...]
    ys = []
    for c in range(rows // chunk):
        sl = slice(c * chunk, (c + 1) * chunk)
        blk = pr_ref[0, sl, :]
        cos = jnp.concatenate([cos_ref[sl, :]] * 3, axis=1)
        sin = jnp.concatenate([sin_ref[sl, :]] * 3, axis=1)
        q = blk[:, 0:C_WIDTH]
        k = blk[:, C_WIDTH:2 * C_WIDTH]
        v = blk[:, 2 * C_WIDTH:3 * C_WIDTH]
        q = q * cos + _swap_halves(q) * sin
        k = (k * cos + _swap_halves(k) * sin) * (HEAD_DIM ** -0.5)
        kdec = k * kd
        pairs = []
        for p in range(C_HEADS // 2):
            ls = slice(p * LANE, (p + 1) * LANE)
            qp, kp, vp = q[:, ls], k[:, ls], v[:, ls]
            kpb = kp.astype(BF16)
            intra = jnp.zeros((chunk, LANE), F32)
            for half, m in ((0, first), (1, second)):
                s = _dot_nt((qp * m).astype(BF16), kpb) * dec_ref[2 * p + half]
                intra = intra + _dot(s.astype(BF16), (vp * m).astype(BF16))
            st = state_ref[p]
            cross = _dot(qp.astype(BF16), st.astype(BF16)) * qd[:, ls]
            kv = _dot_tn(kdec[:, ls].astype(BF16), vp.astype(BF16))
            state_ref[p] = cd_ref[p] * st + kv * bd_ref[...]
            pairs.append(intra + cross)
        ys.append(jnp.concatenate(pairs, axis=1))
    y = jnp.concatenate(ys, axis=0)
    yn = _group_norm(y, gmat_ref[...]) * gng_ref[...]
    g = pr_ref[0, :, 3 * C_WIDTH:4 * C_WIDTH]
    oc_ref[0] = jax.nn.silu(g) * yn
    rout_ref[0] = state_ref[...]


def _ret(pr, cos, sin, dec, qd, kd, cd, bd, gn_g, gmat, r0, chunk, rows):
    nb, t, _ = pr.shape
    npair = C_HEADS // 2
    c2 = lambda b, i: (0, 0)
    c3 = lambda b, i: (0, 0, 0)
    return pl.pallas_call(
        functools.partial(_ret_kernel, chunk=chunk, rows=rows),
        grid=(nb, t // rows),
        in_specs=[pl.BlockSpec((1, rows, SEG_RET), lambda b, i: (b, i, 0)),
                  pl.BlockSpec((rows, LANE), lambda b, i: (i, 0)),
                  pl.BlockSpec((rows, LANE), lambda b, i: (i, 0)),
                  pl.BlockSpec(dec.shape, c3),
                  pl.BlockSpec(qd.shape, c2),
                  pl.BlockSpec(kd.shape, c2),
                  pl.BlockSpec(cd.shape, c3),
                  pl.BlockSpec(bd.shape, c2),
                  pl.BlockSpec((1, C_WIDTH), c2),
                  pl.BlockSpec(gmat.shape, c2),
                  pl.BlockSpec((1, npair, LANE, LANE), lambda b, i: (b, 0, 0, 0))],
        out_specs=[pl.BlockSpec((1, rows, C_WIDTH), lambda b, i: (b, i, 0)),
                   pl.BlockSpec((1, npair, LANE, LANE), lambda b, i: (b, 0, 0, 0))],
        out_shape=[jax.ShapeDtypeStruct((nb, t, C_WIDTH), F32),
                   jax.ShapeDtypeStruct((nb, npair, LANE, LANE), F32)],
        scratch_shapes=[pltpu.VMEM((npair, LANE, LANE), F32)],
        compiler_params=_params(2),
        name="ret",
    )(pr, cos, sin, dec, qd, kd, cd, bd, gn_g, gmat, r0)


def _dsa_kernel(qb_ref, qi_ref, kw_ref, kv_ref, kx_ref, tri_ref, o_ref,
                keys_ref, wb_ref, *, tq, nkb_total, causal, k_sel, n_keys):
    if causal:
        j = pl.program_id(1)
        nkb = j + 1
        row = lax.broadcasted_iota(I32, (tq, 1), 0)
        limit = ((j * tq + row) // CHUNK + 1) * CHUNK
    else:
        nkb = nkb_total
        limit = jnp.full((tq, 1), n_keys, I32)
    lane = lax.broadcasted_iota(I32, (tq, KEY_BLOCK), 1)

    w = kw_ref[0][:, IDX_DIM:IDX_DIM + IDX_HEADS] * (IDX_DIM ** -0.5 * IDX_HEADS ** -0.5)
    for h in range(IDX_HEADS):
        wb_ref[h] = jnp.broadcast_to(w[:, h:h + 1], (tq, KEY_BLOCK))

    def score_body(kb, carry):
        off = pl.multiple_of(kb * KEY_BLOCK, KEY_BLOCK)
        kx = kx_ref[0, pl.ds(off, KEY_BLOCK), :]
        acc = jnp.zeros((tq, KEY_BLOCK), F32)
        for h in range(IDX_HEADS):
            d = _dot_nt(qi_ref[0][:, h * LANE:(h + 1) * LANE], kx)
            acc = acc + jnp.maximum(d, 0.0) * wb_ref[h]
        sc = jnp.where(off + lane < limit, acc, -jnp.inf)
        bits = pltpu.bitcast(sc, I32)
        keys_ref[kb] = bits ^ ((bits >> 31) & 0x7FFFFFFF)
        return carry

    lax.fori_loop(0, nkb, score_body, 0)

    def count(pred):
        def body(kb, acc):
            return acc + jnp.where(pred(keys_ref[kb]), 1.0, 0.0)
        acc = lax.fori_loop(0, nkb, body, jnp.zeros((tq, KEY_BLOCK), F32))
        return jnp.sum(acc, axis=1, keepdims=True)

    def bit_body(i, carry):
        t_off, cnt_t = carry
        cand_off = t_off | jnp.left_shift(jnp.int32(1), 31 - i)
        cand = cand_off ^ INT_MIN
        cnt = count(lambda key: key >= cand)
        take = cnt >= k_sel
        return jnp.where(take, cand_off, t_off), jnp.where(take, cnt, cnt_t)

    n_scored = jnp.full((tq, 1), nkb * KEY_BLOCK, I32).astype(F32)
    t_off, cnt_t = lax.fori_loop(0, 32, bit_body, (jnp.zeros((tq, 1), I32), n_scored))
    thr = t_off ^ INT_MIN

    tied = jnp.logical_and(cnt_t > k_sel, thr > NEG_INF_KEY)
    any_tied = jnp.max(jnp.where(tied, 1.0, 0.0)) > 0.0

    @pl.when(any_tied)
    def _():
        need = k_sel - count(lambda key: key > thr)

        def body(kb, before):
            key = keys_ref[kb]
            eq = key == thr
            eqf = jnp.where(eq, 1.0, 0.0)
            prefix = _dot(eqf.astype(BF16), tri_ref[...]) + before
            drop = jnp.logical_and(eq, prefix >= need)
            keys_ref[kb] = jnp.where(drop, key - 1, key)
            return before + jnp.sum(eqf, axis=1, keepdims=True)

        lax.fori_loop(0, nkb, body, jnp.zeros((tq, 1), F32))

    thr = jnp.maximum(thr, NEG_INF_KEY + 1)

    def bias_body(kb, carry):
        sel = keys_ref[kb] >= thr
        keys_ref[kb] = pltpu.bitcast(jnp.where(sel, 0.0, -jnp.inf), I32)
        return carry

    lax.fori_loop(0, nkb, bias_body, 0)

    outs = []
    for h in range(B_HEADS):
        qh = qb_ref[0][:, h * LANE:(h + 1) * LANE]

        def logits(kb):
            off = pl.multiple_of(kb * KEY_BLOCK, KEY_BLOCK)
            kvb = kv_ref[0, pl.ds(off, KEY_BLOCK), :]
            return _dot_nt(qh, kvb) + pltpu.bitcast(keys_ref[kb], F32), kvb

        def max_body(kb, m):
            return jnp.maximum(m, logits(kb)[0])

        m_lane = lax.fori_loop(0, nkb, max_body, jnp.full((tq, KEY_BLOCK), -jnp.inf, F32))
        m_row = jnp.max(m_lane, axis=1, keepdims=True)

        def pv_body(kb, carry):
            l_lane, acc = carry
            lg, kvb = logits(kb)
            p = jnp.exp(lg - m_row)
            return l_lane + p, acc + _dot(p.astype(BF16), kvb)

        l_lane, acc = lax.fori_loop(
            0, nkb, pv_body, (jnp.zeros((tq, KEY_BLOCK), F32), jnp.zeros((tq, LANE), F32)))
        outs.append(acc / jnp.sum(l_lane, axis=1, keepdims=True))

    for p in range(B_HEADS // 2):
        even = pltpu.roll(outs[2 * p], HEAD_DIM, axis=1)
        o_ref[0, :, p * LANE:(p + 1) * LANE] = jnp.where(lane < HEAD_DIM, even, outs[2 * p + 1])


def _dsa(qb, qi, kw, kvb, kxb, tri, tq, causal, n_keys):
    nb, t, _ = qb.shape
    nk = kvb.shape[1]
    nkb_total = nk // KEY_BLOCK
    kernel = functools.partial(_dsa_kernel, tq=tq, nkb_total=nkb_total, causal=causal,
                               k_sel=min(TOPK_MAX, n_keys // 4), n_keys=n_keys)
    return pl.pallas_call(
        kernel,
        grid=(nb, t // tq),
        in_specs=[pl.BlockSpec((1, tq, SEG_QB), lambda b, j: (b, j, 0)),
                  pl.BlockSpec((1, tq, SEG_QI), lambda b, j: (b, j, 0)),
                  pl.BlockSpec((1, tq, SEG_KW), lambda b, j: (b, j, 0)),
                  pl.BlockSpec((1, nk, SEG_KV), lambda b, j: (b, 0, 0)),
                  pl.BlockSpec((1, nk, SEG_KW), lambda b, j: (b, 0, 0)),
                  pl.BlockSpec(tri.shape, lambda b, j: (0, 0))],
        out_specs=pl.BlockSpec((1, tq, B_WIDTH), lambda b, j: (b, j, 0)),
        out_shape=jax.ShapeDtypeStruct((nb, t, B_WIDTH), F32),
        scratch_shapes=[pltpu.VMEM((nkb_total, tq, KEY_BLOCK), I32),
                        pltpu.VMEM((IDX_HEADS, tq, KEY_BLOCK), F32)],
        compiler_params=_params(2),
        name="dsa",
    )(qb, qi, kw, kvb, kxb, tri)


def _ffn_kernel(*refs, tm, alpha, streaming, seq):
    if streaming:
        (x_ref, oa_ref, ob_ref, oc_ref, wo_ref, g1_ref, b1_ref, wg_ref, wu_ref, cw_ref, cb_ref, wd_ref,
         g2_ref, b2_ref, y_ref, cs_ref, x1_ref, x1b_ref, acc_ref, carry_ref) = refs
    else:
        (x_ref, oa_ref, ob_ref, oc_ref, wo_ref, g1_ref, b1_ref, wg_ref, wu_ref, cw_ref, cb_ref, wd_ref,
         g2_ref, b2_ref, p1_ref, p2_ref, y_ref, cs_ref, x1_ref, x1b_ref, acc_ref) = refs
    f = pl.program_id(1)
    nf = pl.num_programs(1)

    @pl.when(f == 0)
    def _():
        mix = (_dot(oa_ref[...].astype(BF16), wo_ref[0:A_WIDTH, :])
               + _dot(ob_ref[...].astype(BF16), wo_ref[A_WIDTH:A_WIDTH + B_WIDTH, :])
               + _dot(oc_ref[...].astype(BF16), wo_ref[A_WIDTH + B_WIDTH:, :]))
        x1 = _layer_norm(alpha * x_ref[...] + mix, g1_ref[...], b1_ref[...])
        x1_ref[...] = x1
        x1b_ref[...] = x1.astype(BF16)
        acc_ref[...] = jnp.zeros_like(acc_ref)

    xb = x1b_ref[...]
    hg = _dot(xb, wg_ref[...])
    hu = _dot(xb, wu_ref[...])
    tf = hg.shape[1]
    row = lax.broadcasted_iota(I32, (tm, tf), 0)
    h1 = pltpu.roll(hg, 1, axis=0)
    h2 = pltpu.roll(hg, 2, axis=0)
    if streaming:
        @pl.when((pl.program_id(0) * tm) % seq == 0)
        def _():
            carry_ref[f] = jnp.zeros((8, tf), F32)

        prev = carry_ref[f]
        c0 = prev[6:7, :]
        c1 = prev[7:8, :]
        h1 = jnp.where(row == 0, c1, h1)
        h2 = jnp.where(row == 0, c0, jnp.where(row == 1, c1, h2))
        carry_ref[f] = hg[tm - 8:, :]
        cs_ref[0] = hg[tm - 8:, :]
    else:
        pos = row % seq
        h1 = jnp.where(pos == 0, p1_ref[...], h1)
        h2 = jnp.where(pos < 2, p2_ref[...], h2)
        cs_ref[...] = hg
    cw = cw_ref[...]
    conv = cb_ref[...] + cw[0:1, :] * h2 + cw[1:2, :] * h1 + cw[2:3, :] * hg
    act = jax.nn.gelu(conv) * hu
    acc_ref[...] += _dot(act.astype(BF16), wd_ref[...])

    @pl.when(f == nf - 1)
    def _():
        y_ref[...] = _layer_norm(alpha * x1_ref[...] + acc_ref[...], g2_ref[...], b2_ref[...])


def _ffn(x2d, oa, ob, oc, lw, tm, tf, seq, prev=None):
    t = x2d.shape[0]
    nf = D_FF // tf
    streaming = prev is None
    alpha = lw["alpha"]
    row = lambda i, f: (i, 0)
    const = lambda i, f: (0, 0)
    in_specs = [pl.BlockSpec((tm, D_MODEL), row),
                pl.BlockSpec((tm, A_WIDTH), row),
                pl.BlockSpec((tm, B_WIDTH), row),
                pl.BlockSpec((tm, C_WIDTH), row),
                pl.BlockSpec((D_MODEL, D_MODEL), const),
                pl.BlockSpec((1, D_MODEL), const),
                pl.BlockSpec((1, D_MODEL), const),
                pl.BlockSpec((D_MODEL, tf), lambda i, f: (0, f)),
                pl.BlockSpec((D_MODEL, tf), lambda i, f: (0, f)),
                pl.BlockSpec((CONV_W, tf), lambda i, f: (0, f)),
                pl.BlockSpec((1, tf), lambda i, f: (0, f)),
                pl.BlockSpec((tf, D_MODEL), lambda i, f: (f, 0)),
                pl.BlockSpec((1, D_MODEL), const),
                pl.BlockSpec((1, D_MODEL), const)]
    args = [x2d, oa, ob, oc, lw["w_out"], lw["ln1_g"], lw["ln1_b"], lw["w_gate"], lw["w_up"],
            lw["conv_w"], lw["conv_b"], lw["w_down"], lw["ln2_g"], lw["ln2_b"]]
    scratch = [pltpu.VMEM((tm, D_MODEL), F32), pltpu.VMEM((tm, D_MODEL), BF16),
               pltpu.VMEM((tm, D_MODEL), F32)]
    if streaming:
        cs_spec = pl.BlockSpec((1, 8, tf), lambda i, f: (i, 0, f))
        cs_shape = jax.ShapeDtypeStruct((t // tm, 8, D_FF), F32)
        scratch.append(pltpu.VMEM((nf, 8, tf), F32))
    else:
        in_specs += [pl.BlockSpec((tm, tf), lambda i, f: (i, f))] * 2
        args += list(prev)
        cs_spec = pl.BlockSpec((tm, tf), lambda i, f: (i, f))
        cs_shape = jax.ShapeDtypeStruct((t, D_FF), F32)
    return pl.pallas_call(
        functools.partial(_ffn_kernel, tm=tm, alpha=alpha, streaming=streaming, seq=seq),
        grid=(t // tm, nf),
        in_specs=in_specs,
        out_specs=[pl.BlockSpec((tm, D_MODEL), row), cs_spec],
        out_shape=[jax.ShapeDtypeStruct((t, D_MODEL), F32), cs_shape],
        scratch_shapes=scratch,
        compiler_params=_params(2),
        name="ffn",
    )(*args)


def _pad_cols(w, width):
    return jnp.pad(w, ((0, 0), (0, width - w.shape[1])))


def _proj_weight(w_in):
    sizes = (A_WIDTH, A_WIDTH, B_WIDTH, HEAD_DIM, HEAD_DIM, IDX_HEADS * IDX_DIM, IDX_DIM, IDX_HEADS,
             C_WIDTH, C_WIDTH, C_WIDTH, C_WIDTH)
    splits = np.cumsum(sizes)[:-1].tolist()
    ua, va, qb, kb, vb, qib, kib, wib, qc, kc, vc, gc = jnp.split(w_in, splits, axis=1)
    qb = qb * (HEAD_DIM ** -0.5)
    cols = [ua, va]
    cols += [_pad_cols(qb[:, h * HEAD_DIM:(h + 1) * HEAD_DIM], LANE) for h in range(B_HEADS)]
    cols += [qc, kc, vc, gc]
    cols += [_pad_cols(qib[:, h * IDX_DIM:(h + 1) * IDX_DIM], LANE) for h in range(IDX_HEADS)]
    cols += [kb, vb, _pad_cols(jnp.concatenate([kib, wib], axis=1), LANE)]
    return jnp.concatenate(cols, axis=1).astype(BF16)


def _group_matrix(width):
    g = np.arange(width) // HEAD_DIM
    return jnp.asarray((g[:, None] == g[None, :]).astype(np.float32) / HEAD_DIM, BF16)


def _rotary_tables(pos):
    half = HEAD_DIM // 2
    freqs = ROPE_BASE ** (-jnp.arange(half, dtype=F32) / half)
    ang = pos.astype(F32)[:, None] * freqs
    cos, sin = jnp.cos(ang), jnp.sin(ang)
    cos = jnp.concatenate([cos, cos, cos, cos], axis=1)
    sin = jnp.concatenate([-sin, sin, -sin, sin], axis=1)
    return cos, sin


def _retention_tables(c):
    log_g = jnp.log(1.0 - 2.0 ** (-5.0 - jnp.arange(C_HEADS, dtype=F32)))
    i = jnp.arange(c, dtype=F32)
    diff = i[:, None] - i[None, :]
    dec = jnp.where(diff >= 0, jnp.exp(jnp.maximum(diff, 0.0)[None] * log_g[:, None, None]), 0.0)
    qd = jnp.repeat(jnp.exp((i + 1)[None] * log_g[:, None]).T, HEAD_DIM, axis=1)
    kd = jnp.repeat(jnp.exp((c - 1 - i)[None] * log_g[:, None]).T, HEAD_DIM, axis=1)
    cdl = jnp.repeat(jnp.exp(c * log_g), HEAD_DIM).reshape(C_HEADS // 2, LANE)
    cd = jnp.broadcast_to(cdl[:, :, None], (C_HEADS // 2, LANE, LANE))
    blk = np.arange(LANE) // HEAD_DIM
    bd = jnp.asarray((blk[:, None] == blk[None, :]).astype(np.float32))
    return dec, qd, kd, cd, bd


def _pair_states(r):
    n = r.shape[0]
    r = r.reshape(n, C_HEADS // 2, 2, HEAD_DIM, HEAD_DIM).astype(F32)
    z = jnp.zeros_like(r[:, :, 0])
    top = jnp.concatenate([r[:, :, 0], z], axis=-1)
    bot = jnp.concatenate([z, r[:, :, 1]], axis=-1)
    return jnp.concatenate([top, bot], axis=-2)


def _unpair_states(s):
    a = s[:, :, :HEAD_DIM, :HEAD_DIM]
    b = s[:, :, HEAD_DIM:, HEAD_DIM:]
    n = s.shape[0]
    return jnp.stack([a, b], axis=2).reshape(n, C_HEADS, HEAD_DIM, HEAD_DIM)


def _mixa_tables(ws, bs, c):
    tril = jnp.tril(jnp.ones((c, c), ws.dtype))
    w = ws[:, :c, :c] * tril
    wcat = jnp.concatenate([w[g] for g in range(A_GROUPS)], axis=1).astype(BF16)
    bias = jnp.repeat(bs[:, :c].T, HEAD_DIM, axis=1)
    return wcat, bias


def _layer_weights(l, depth, w_out, ln1_g, ln1_b, w_gate, w_up, conv_w, conv_b, w_down, ln2_g, ln2_b):
    return dict(alpha=float((2 * depth) ** 0.25),
                w_out=w_out[l].astype(BF16), ln1_g=ln1_g[l][None], ln1_b=ln1_b[l][None],
                w_gate=w_gate[l].astype(BF16), w_up=w_up[l].astype(BF16), conv_w=conv_w[l],
                conv_b=conv_b[l][None], w_down=w_down[l].astype(BF16),
                ln2_g=ln2_g[l][None], ln2_b=ln2_b[l][None])


def kernel(x_prompt, x_sample, cache_b_k, cache_b_v, cache_b_kidx, state_ret, state_ffn_conv,
           w_in, a_ln_g, a_ln_b, a_ws, a_bs, c_gn_g, w_out, ln1_g, ln1_b,
           w_gate, w_up, conv_w, conv_b, w_down, ln2_g, ln2_b):
    depth = w_in.shape[0]
    bp, s, _ = x_prompt.shape
    bs_, t, _ = x_sample.shape
    past = cache_b_k.shape[2]
    l_keys = past + t
    ts = bs_ * t
    tm_p = 512
    tm_s = ts if ts <= 512 else 512
    tf = 256
    assert s % tm_p == 0 and s % A_CHUNK == 0 and ts % tm_s == 0 and tm_s % t == 0 and t >= 2
    nk_s = -(-l_keys // KEY_BLOCK) * KEY_BLOCK

    gmat_a = _group_matrix(A_WIDTH)
    gmat_c = _group_matrix(C_WIDTH)
    tri = jnp.asarray(np.triu(np.ones((KEY_BLOCK, KEY_BLOCK), np.float32), 1), BF16)
    cos_p, sin_p = _rotary_tables(jnp.arange(s))
    cos_s, sin_s = _rotary_tables(past + jnp.arange(t))
    ret_p = _retention_tables(CHUNK)
    ret_s = _retention_tables(t)

    xp = x_prompt.reshape(bp * s, D_MODEL)
    xs = x_sample.reshape(ts, D_MODEL)
    outs = {k: [] for k in ("kp", "vp", "kip", "rp", "cp", "ks", "vs", "kis", "rs", "cs", "avs")}
    for l in range(depth):
        w_proj = _proj_weight(w_in[l])
        lw = _layer_weights(l, depth, w_out, ln1_g, ln1_b, w_gate, w_up, conv_w, conv_b, w_down, ln2_g, ln2_b)
        ln_g, ln_b, gn_g = a_ln_g[l][None], a_ln_b[l][None], c_gn_g[l][None]

        pa, qb, pr, qi, kv, kw, kvb, kwb = _proj(xp, w_proj, tm_p)
        wcat, bias = _mixa_tables(a_ws[l], a_bs[l], A_CHUNK)
        oa, _ = _mixa(pa, wcat, bias, ln_g, ln_b, gmat_a, A_CHUNK, 512)
        r0 = jnp.zeros((bp, C_HEADS // 2, LANE, LANE), F32)
        oc, r_p = _ret(pr.reshape(bp, s, SEG_RET), cos_p, sin_p, *ret_p, gn_g, gmat_c, r0, CHUNK, 512)
        ob = _dsa(qb.reshape(bp, s, SEG_QB), qi.reshape(bp, s, SEG_QI), kw.reshape(bp, s, SEG_KW),
                  kvb.reshape(bp, s, SEG_KV), kwb.reshape(bp, s, SEG_KW), tri, KEY_BLOCK, True, s)
        xp, c_p = _ffn(xp, oa, ob.reshape(bp * s, B_WIDTH), oc.reshape(bp * s, C_WIDTH), lw, tm_p, tf, s)
        kv3 = kv.reshape(bp, s, SEG_KV)
        outs["kp"].append(kv3[..., :HEAD_DIM])
        outs["vp"].append(kv3[..., HEAD_DIM:])
        outs["kip"].append(kw.reshape(bp, s, SEG_KW)[..., :IDX_DIM])
        outs["rp"].append(_unpair_states(r_p))
        outs["cp"].append(c_p[s // tm_p - 1::s // tm_p, 8 - (CONV_W - 1):, :])

        pa, qb, pr, qi, kv, kw, kvb, kwb = _proj(xs, w_proj, tm_s)
        wcat, bias = _mixa_tables(a_ws[l], a_bs[l], t)
        oa, av = _mixa(pa, wcat, bias, ln_g, ln_b, gmat_a, t, tm_s)
        r0 = _pair_states(state_ret[l])
        rows_s = t
        oc, r_s = _ret(pr.reshape(bs_, t, SEG_RET), cos_s, sin_s, *ret_s, gn_g, gmat_c, r0, t, rows_s)
        pad = nk_s - l_keys
        kv_full = jnp.concatenate(
            [jnp.concatenate([cache_b_k[l], cache_b_v[l]], axis=-1).astype(BF16),
             kvb.reshape(bs_, t, SEG_KV), jnp.zeros((bs_, pad, SEG_KV), BF16)], axis=1)
        kx_full = jnp.concatenate(
            [jnp.pad(cache_b_kidx[l], ((0, 0), (0, 0), (0, SEG_KW - IDX_DIM))).astype(BF16),
             kwb.reshape(bs_, t, SEG_KW), jnp.zeros((bs_, pad, SEG_KW), BF16)], axis=1)
        ob = _dsa(qb.reshape(bs_, t, SEG_QB), qi.reshape(bs_, t, SEG_QI), kw.reshape(bs_, t, SEG_KW),
                  kv_full, kx_full, tri, t, False, l_keys)
        cprev = state_ffn_conv[l]
        zeros = jnp.zeros((bs_, t - 2, D_FF), F32)
        p1 = jnp.concatenate([cprev[:, 1:2], zeros, zeros[:, :1]], axis=1).reshape(ts, D_FF)
        p2 = jnp.concatenate([cprev, zeros], axis=1).reshape(ts, D_FF)
        xs, hg_s = _ffn(xs, oa, ob.reshape(ts, B_WIDTH), oc.reshape(ts, C_WIDTH), lw, tm_s, tf, t, (p1, p2))
        kv3 = kv.reshape(bs_, t, SEG_KV)
        outs["ks"].append(kv3[..., :HEAD_DIM])
        outs["vs"].append(kv3[..., HEAD_DIM:])
        outs["kis"].append(kw.reshape(bs_, t, SEG_KW)[..., :IDX_DIM])
        outs["rs"].append(_unpair_states(r_s))
        outs["cs"].append(hg_s.reshape(bs_, t, D_FF)[:, t - (CONV_W - 1):, :])
        outs["avs"].append(av.reshape(bs_, t, A_WIDTH))

    st = lambda k: jnp.stack(outs[k])
    return (xp.reshape(bp, s, D_MODEL), xs.reshape(bs_, t, D_MODEL),
            st("kp"), st("vp"), st("kip"), st("rp"), st("cp"),
            st("ks"), st("vs"), st("kis"), st("rs"), st("cs"), st("avs"))
```

```python
import functools

import numpy as np
import jax
import jax.numpy as jnp
from jax import lax
from jax.experimental import pallas as pl
from jax.experimental.pallas import tpu as pltpu

F32 = jnp.float32
BF16 = jnp.bfloat16
I32 = jnp.int32

D_MODEL = 1024
CHUNK = 64
HEAD_DIM = 64
A_GROUPS = 4
C_HEADS = 6
B_HEADS = 6
A_WIDTH = A_GROUPS * HEAD_DIM
B_WIDTH = B_HEADS * HEAD_DIM
C_WIDTH = C_HEADS * HEAD_DIM
A_CHUNK = 128
IDX_HEADS = 8
IDX_DIM = 32
TOPK_MAX = 256
ROPE_BASE = 10000.0
D_FF = 2816
CONV_W = 3
LN_EPS = 1e-5

LANE = 128
QUERY_BLOCK = 128
KEY_SUPER = 4 * LANE
VMEM_LIMIT = 48 * 1024 * 1024

SEG_A = 2 * A_WIDTH
SEG_QB = B_HEADS * LANE
SEG_RET = 4 * C_WIDTH
SEG_QI = IDX_HEADS * LANE
SEG_KV = LANE
SEG_KW = LANE
NEG_INF_KEY = np.int32(np.uint32(0x807FFFFF).astype(np.int64) - (1 << 32))
INT_MIN = np.int32(-(2 ** 31))


def _dot(a, b):
    return jnp.dot(a, b, preferred_element_type=F32)


def _dot_nt(a, b):
    return lax.dot_general(a, b, (((1,), (1,)), ((), ())), preferred_element_type=F32)


def _dot_tn(a, b):
    return lax.dot_general(a, b, (((0,), (0,)), ((), ())), preferred_element_type=F32)


def _split_dot(x, m):
    hi = x.astype(BF16)
    lo = (x - hi.astype(F32)).astype(BF16)
    return _dot(hi, m) + _dot(lo, m)


def _group_norm(x, gmat):
    mu = _split_dot(x, gmat)
    d = x - mu
    var = _split_dot(d * d, gmat)
    return d * lax.rsqrt(var + LN_EPS)


def _layer_norm(x, g, b):
    mu = jnp.mean(x, axis=-1, keepdims=True)
    d = x - mu
    var = jnp.mean(d * d, axis=-1, keepdims=True)
    return d * lax.rsqrt(var + LN_EPS) * g + b


def _params(n_axes):
    return pltpu.CompilerParams(dimension_semantics=("arbitrary",) * n_axes,
                                vmem_limit_bytes=VMEM_LIMIT)


def _proj_kernel(x_ref, w_ref, a_ref, qb_ref, ret_ref, qi_ref, kv_ref, kw_ref, kvb_ref, kwb_ref):
    xb = x_ref[...].astype(BF16)
    o = 0
    a_ref[...] = _dot(xb, w_ref[:, o:o + SEG_A]); o += SEG_A
    qb_ref[...] = _dot(xb, w_ref[:, o:o + SEG_QB]).astype(BF16); o += SEG_QB
    ret_ref[...] = _dot(xb, w_ref[:, o:o + SEG_RET]); o += SEG_RET
    qi_ref[...] = _dot(xb, w_ref[:, o:o + SEG_QI]).astype(BF16); o += SEG_QI
    kv = _dot(xb, w_ref[:, o:o + SEG_KV]); o += SEG_KV
    kw = _dot(xb, w_ref[:, o:o + SEG_KW])
    kv_ref[...] = kv
    kw_ref[...] = kw
    kvb_ref[...] = kv.astype(BF16)
    kwb_ref[...] = kw.astype(BF16)


def _proj(x2d, w_proj, tm):
    t = x2d.shape[0]
    n_cols = w_proj.shape[1]
    widths = (SEG_A, SEG_QB, SEG_RET, SEG_QI, SEG_KV, SEG_KW, SEG_KV, SEG_KW)
    dtypes = (F32, BF16, F32, BF16, F32, F32, BF16, BF16)
    return pl.pallas_call(
        _proj_kernel,
        grid=(t // tm,),
        in_specs=[pl.BlockSpec((tm, D_MODEL), lambda i: (i, 0)),
                  pl.BlockSpec((D_MODEL, n_cols), lambda i: (0, 0))],
        out_specs=[pl.BlockSpec((tm, w), lambda i: (i, 0)) for w in widths],
        out_shape=[jax.ShapeDtypeStruct((t, w), d) for w, d in zip(widths, dtypes)],
        compiler_params=_params(1),
        name="proj",
    )(x2d, w_proj)


def _mixa_kernel(pa_ref, wcat_ref, bias_ref, g_ref, b_ref, gmat_ref, oa_ref, av_ref, *, chunk, rows):
    pa = pa_ref[...]
    u = jax.nn.gelu(pa[:, :A_WIDTH])
    v = jax.nn.gelu(pa[:, A_WIDTH:])
    vn = _group_norm(v, gmat_ref[...]) * g_ref[...] + b_ref[...]
    av_ref[...] = vn
    group = lax.broadcasted_iota(I32, (chunk, A_WIDTH), 1) // HEAD_DIM
    wcat = wcat_ref[...]
    bias = bias_ref[...]
    for c in range(rows // chunk):
        vc = vn[c * chunk:(c + 1) * chunk, :]
        stacked = jnp.concatenate(
            [jnp.where(group == g, vc, 0.0).astype(BF16) for g in range(A_GROUPS)], axis=0)
        s = _dot(wcat, stacked) + bias
        oa_ref[c * chunk:(c + 1) * chunk, :] = u[c * chunk:(c + 1) * chunk, :] * s


def _mixa(pa, wcat, bias, ln_g, ln_b, gmat, chunk, rows):
    t = pa.shape[0]
    const = lambda i: (0, 0)
    return pl.pallas_call(
        functools.partial(_mixa_kernel, chunk=chunk, rows=rows),
        grid=(t // rows,),
        in_specs=[pl.BlockSpec((rows, SEG_A), lambda i: (i, 0)),
                  pl.BlockSpec(wcat.shape, const),
                  pl.BlockSpec(bias.shape, const),
                  pl.BlockSpec((1, A_WIDTH), const),
                  pl.BlockSpec((1, A_WIDTH), const),
                  pl.BlockSpec(gmat.shape, const)],
        out_specs=[pl.BlockSpec((rows, A_WIDTH), lambda i: (i, 0)),
                   pl.BlockSpec((rows, A_WIDTH), lambda i: (i, 0))],
        out_shape=[jax.ShapeDtypeStruct((t, A_WIDTH), F32),
                   jax.ShapeDtypeStruct((t, A_WIDTH), F32)],
        compiler_params=_params(1),
        name="mixa",
    )(pa, wcat, bias, ln_g, ln_b, gmat)


def _swap_halves(x):
    n = x.shape[1]
    lane = lax.broadcasted_iota(I32, x.shape, 1)
    fwd = pltpu.roll(x, n - HEAD_DIM // 2, axis=1)
    bwd = pltpu.roll(x, HEAD_DIM // 2, axis=1)
    return jnp.where(lane % HEAD_DIM < HEAD_DIM // 2, fwd, bwd)


def _ret_kernel(pr_ref, cos_ref, sin_ref, dec_ref, qd_ref, kd_ref, cd_ref, bd_ref, gng_ref, gmat_ref,
                r0_ref, oc_ref, rout_ref, state_ref, *, chunk, rows):
    @pl.when(pl.program_id(1) == 0)
    def _():
        state_ref[...] = r0_ref[0]

    lane = lax.broadcasted_iota(I32, (chunk, LANE), 1)
    first = (lane < HEAD_DIM).astype(F32)
    second = 1.0 - first
    qd = qd_ref[...]
    kd = kd_ref[...]
    ys = []
    for c in range(rows // chunk):
        sl = slice(c * chunk, (c + 1) * chunk)
        blk = pr_ref[0, sl, :]
        cos = jnp.concatenate([cos_ref[sl, :]] * 3, axis=1)
        sin = jnp.concatenate([sin_ref[sl, :]] * 3, axis=1)
        q = blk[:, 0:C_WIDTH]
        k = blk[:, C_WIDTH:2 * C_WIDTH]
        v = blk[:, 2 * C_WIDTH:3 * C_WIDTH]
        q = q * cos + _swap_halves(q) * sin
        k = (k * cos + _swap_halves(k) * sin) * (HEAD_DIM ** -0.5)
        kdec = k * kd
        pairs = []
        for p in range(C_HEADS // 2):
            ls = slice(p * LANE, (p + 1) * LANE)
            qp, kp, vp = q[:, ls], k[:, ls], v[:, ls]
            kpb = kp.astype(BF16)
            intra = jnp.zeros((chunk, LANE), F32)
            for half, m in ((0, first), (1, second)):
                s = _dot_nt((qp * m).astype(BF16), kpb) * dec_ref[2 * p + half]
                intra = intra + _dot(s.astype(BF16), (vp * m).astype(BF16))
            st = state_ref[p]
            cross = _dot(qp.astype(BF16), st.astype(BF16)) * qd[:, ls]
            kv = _dot_tn(kdec[:, ls].astype(BF16), vp.astype(BF16))
            state_ref[p] = cd_ref[p] * st + kv * bd_ref[...]
            pairs.append(intra + cross)
        ys.append(jnp.concatenate(pairs, axis=1))
    y = jnp.concatenate(ys, axis=0)
    yn = _group_norm(y, gmat_ref[...]) * gng_ref[...]
    g = pr_ref[0, :, 3 * C_WIDTH:4 * C_WIDTH]
    oc_ref[0] = jax.nn.silu(g) * yn
    rout_ref[0] = state_ref[...]


def _ret(pr, cos, sin, dec, qd, kd, cd, bd, gn_g, gmat, r0, chunk, rows):
    nb, t, _ = pr.shape
    npair = C_HEADS // 2
    c2 = lambda b, i: (0, 0)
    c3 = lambda b, i: (0, 0, 0)
    return pl.pallas_call(
        functools.partial(_ret_kernel, chunk=chunk, rows=rows),
        grid=(nb, t // rows),
        in_specs=[pl.BlockSpec((1, rows, SEG_RET), lambda b, i: (b, i, 0)),
                  pl.BlockSpec((rows, LANE), lambda b, i: (i, 0)),
                  pl.BlockSpec((rows, LANE), lambda b, i: (i, 0)),
                  pl.BlockSpec(dec.shape, c3),
                  pl.BlockSpec(qd.shape, c2),
                  pl.BlockSpec(kd.shape, c2),
                  pl.BlockSpec(cd.shape, c3),
                  pl.BlockSpec(bd.shape, c2),
                  pl.BlockSpec((1, C_WIDTH), c2),
                  pl.BlockSpec(gmat.shape, c2),
                  pl.BlockSpec((1, npair, LANE, LANE), lambda b, i: (b, 0, 0, 0))],
        out_specs=[pl.BlockSpec((1, rows, C_WIDTH), lambda b, i: (b, i, 0)),
                   pl.BlockSpec((1, npair, LANE, LANE), lambda b, i: (b, 0, 0, 0))],
        out_shape=[jax.ShapeDtypeStruct((nb, t, C_WIDTH), F32),
                   jax.ShapeDtypeStruct((nb, npair, LANE, LANE), F32)],
        scratch_shapes=[pltpu.VMEM((npair, LANE, LANE), F32)],
        compiler_params=_params(2),
        name="ret",
    )(pr, cos, sin, dec, qd, kd, cd, bd, gn_g, gmat, r0)


def _lane_blocks(x):
    return [x[:, i * LANE:(i + 1) * LANE] for i in range(x.shape[1] // LANE)]


def _key_to_float(key):
    return pltpu.bitcast(key ^ ((key >> 31) & 0x7FFFFFFF), F32)


def _dsa_kernel(qb_ref, qi_ref, kw_ref, kv_ref, kx_ref, tri_ref, o_ref,
                sc_ref, wb_ref, m_ref, l_ref, acc_ref, *, tq, nsb_total, causal, k_sel, n_keys):
    ks = KEY_SUPER
    nrep = ks // LANE
    if causal:
        j = pl.program_id(1)
        nsb = (j * tq) // ks + 1
        row = lax.broadcasted_iota(I32, (tq, 1), 0)
        limit = ((j * tq + row) // CHUNK + 1) * CHUNK
    else:
        nsb = nsb_total
        limit = jnp.full((tq, 1), n_keys, I32)
    lane = lax.broadcasted_iota(I32, (tq, ks), 1)

    def wide(x):
        return jnp.concatenate([x] * nrep, axis=1)

    qs = jnp.concatenate([qb_ref[0][:, h * LANE:(h + 1) * LANE] for h in range(B_HEADS)], axis=0)
    qis = jnp.concatenate([qi_ref[0][:, h * LANE:(h + 1) * LANE] for h in range(IDX_HEADS)], axis=0)

    w = kw_ref[0][:, IDX_DIM:IDX_DIM + IDX_HEADS] * (IDX_DIM ** -0.5 * IDX_HEADS ** -0.5)
    for h in range(IDX_HEADS):
        wb_ref[h] = jnp.broadcast_to(w[:, h:h + 1], (tq, LANE))

    def score_body(sb, carry):
        off = pl.multiple_of(sb * ks, ks)
        d = _dot_nt(qis, kx_ref[0, pl.ds(off, ks), :])
        acc = jnp.zeros((tq, ks), F32)
        for h in range(IDX_HEADS):
            acc = acc + jnp.maximum(d[h * tq:(h + 1) * tq], 0.0) * wide(wb_ref[h])
        sc_ref[sb] = jnp.where(off + lane < limit, acc, -jnp.inf)
        return carry

    lax.fori_loop(0, nsb, score_body, 0)

    def count(pred):
        def body(sb, acc):
            return acc + sum(_lane_blocks(jnp.where(pred(sc_ref[sb]), 1.0, 0.0)))
        acc = lax.fori_loop(0, nsb, body, jnp.zeros((tq, LANE), F32))
        return jnp.sum(acc, axis=1, keepdims=True)

    def bit_body(i, carry):
        t_off, cnt_t = carry
        cand_off = t_off | jnp.left_shift(jnp.int32(1), 31 - i)
        cand = _key_to_float(cand_off ^ INT_MIN)
        cnt = count(lambda x: x >= cand)
        take = cnt >= k_sel
        return jnp.where(take, cand_off, t_off), jnp.where(take, cnt, cnt_t)

    n_scored = jnp.full((tq, 1), nsb * ks, I32).astype(F32)
    t_off, cnt_t = lax.fori_loop(0, 32, bit_body, (jnp.zeros((tq, 1), I32), n_scored))
    thr_key = t_off ^ INT_MIN
    thr = _key_to_float(jnp.maximum(thr_key, NEG_INF_KEY + 1))

    tied = jnp.logical_and(cnt_t > k_sel, thr_key > NEG_INF_KEY)
    any_tied = jnp.max(jnp.where(tied, 1.0, 0.0)) > 0.0

    @pl.when(any_tied)
    def _():
        need = k_sel - count(lambda x: x > thr)

        def body(sb, before):
            sc = sc_ref[sb]
            eq = sc == thr
            eqf = jnp.where(eq, 1.0, 0.0)
            prefix = _dot(eqf.astype(BF16), tri_ref[...]) + before
            drop = jnp.logical_and(eq, prefix >= need)
            sc_ref[sb] = jnp.where(drop, -jnp.inf, sc)
            return before + jnp.sum(eqf, axis=1, keepdims=True)

        lax.fori_loop(0, nsb, body, jnp.zeros((tq, 1), F32))

    def bias_body(sb, carry):
        sc_ref[sb] = jnp.where(sc_ref[sb] >= thr, 0.0, -jnp.inf)
        return carry

    lax.fori_loop(0, nsb, bias_body, 0)

    def logits(sb):
        off = pl.multiple_of(sb * ks, ks)
        kvb = kv_ref[0, pl.ds(off, ks), :]
        return _dot_nt(qs, kvb), kvb

    m_ref[...] = jnp.full(m_ref.shape, -jnp.inf, F32)

    def max_body(sb, carry):
        s, _ = logits(sb)
        bias = sc_ref[sb]
        for h in range(B_HEADS):
            blocks = _lane_blocks(s[h * tq:(h + 1) * tq] + bias)
            m_ref[h] = functools.reduce(jnp.maximum, blocks, m_ref[h])
        return carry

    lax.fori_loop(0, nsb, max_body, 0)
    for h in range(B_HEADS):
        m_ref[h] = jnp.broadcast_to(jnp.max(m_ref[h], axis=1, keepdims=True), (tq, LANE))
    l_ref[...] = jnp.zeros(l_ref.shape, F32)
    acc_ref[...] = jnp.zeros(acc_ref.shape, F32)

    def pv_body(sb, carry):
        s, kvb = logits(sb)
        bias = sc_ref[sb]
        ps = []
        for h in range(B_HEADS):
            p = jnp.exp(s[h * tq:(h + 1) * tq] + bias - wide(m_ref[h]))
            l_ref[h] += sum(_lane_blocks(p))
            ps.append(p.astype(BF16))
        acc_ref[...] += _dot(jnp.concatenate(ps, axis=0), kvb)
        return carry

    lax.fori_loop(0, nsb, pv_body, 0)

    outs = [acc_ref[h * tq:(h + 1) * tq] / jnp.sum(l_ref[h], axis=1, keepdims=True) for h in range(B_HEADS)]
    lane_head = lax.broadcasted_iota(I32, (tq, LANE), 1)
    for p in range(B_HEADS // 2):
        even = pltpu.roll(outs[2 * p], HEAD_DIM, axis=1)
        o_ref[0, :, p * LANE:(p + 1) * LANE] = jnp.where(lane_head < HEAD_DIM, even, outs[2 * p + 1])


def _dsa(qb, qi, kw, kvb, kxb, tri, tq, causal, n_keys):
    nb, t, _ = qb.shape
    nk = kvb.shape[1]
    nsb_total = nk // KEY_SUPER
    kernel = functools.partial(_dsa_kernel, tq=tq, nsb_total=nsb_total, causal=causal,
                               k_sel=min(TOPK_MAX, n_keys // 4), n_keys=n_keys)
    return pl.pallas_call(
        kernel,
        grid=(nb, t // tq),
        in_specs=[pl.BlockSpec((1, tq, SEG_QB), lambda b, j: (b, j, 0)),
                  pl.BlockSpec((1, tq, SEG_QI), lambda b, j: (b, j, 0)),
                  pl.BlockSpec((1, tq, SEG_KW), lambda b, j: (b, j, 0)),
                  pl.BlockSpec((1, nk, SEG_KV), lambda b, j: (b, 0, 0)),
                  pl.BlockSpec((1, nk, SEG_KW), lambda b, j: (b, 0, 0)),
                  pl.BlockSpec(tri.shape, lambda b, j: (0, 0))],
        out_specs=pl.BlockSpec((1, tq, B_WIDTH), lambda b, j: (b, j, 0)),
        out_shape=jax.ShapeDtypeStruct((nb, t, B_WIDTH), F32),
        scratch_shapes=[pltpu.VMEM((nsb_total, tq, KEY_SUPER), F32),
                        pltpu.VMEM((IDX_HEADS, tq, LANE), F32),
                        pltpu.VMEM((B_HEADS, tq, LANE), F32),
                        pltpu.VMEM((B_HEADS, tq, LANE), F32),
                        pltpu.VMEM((B_HEADS * tq, LANE), F32)],
        compiler_params=_params(2),
        name="dsa",
    )(qb, qi, kw, kvb, kxb, tri)


def _ffn_kernel(*refs, tm, alpha, streaming, seq):
    if streaming:
        (x_ref, oa_ref, ob_ref, oc_ref, wo_ref, g1_ref, b1_ref, wg_ref, wu_ref, cw_ref, cb_ref, wd_ref,
         g2_ref, b2_ref, y_ref, cs_ref, x1_ref, x1b_ref, acc_ref, carry_ref) = refs
    else:
        (x_ref, oa_ref, ob_ref, oc_ref, wo_ref, g1_ref, b1_ref, wg_ref, wu_ref, cw_ref, cb_ref, wd_ref,
         g2_ref, b2_ref, p1_ref, p2_ref, y_ref, cs_ref, x1_ref, x1b_ref, acc_ref) = refs
    f = pl.program_id(1)
    nf = pl.num_programs(1)

    @pl.when(f == 0)
    def _():
        mix = (_dot(oa_ref[...].astype(BF16), wo_ref[0:A_WIDTH, :])
               + _dot(ob_ref[...].astype(BF16), wo_ref[A_WIDTH:A_WIDTH + B_WIDTH, :])
               + _dot(oc_ref[...].astype(BF16), wo_ref[A_WIDTH + B_WIDTH:, :]))
        x1 = _layer_norm(alpha * x_ref[...] + mix, g1_ref[...], b1_ref[...])
        x1_ref[...] = x1
        x1b_ref[...] = x1.astype(BF16)
        acc_ref[...] = jnp.zeros_like(acc_ref)

    xb = x1b_ref[...]
    hg = _dot(xb, wg_ref[...])
    hu = _dot(xb, wu_ref[...])
    tf = hg.shape[1]
    row = lax.broadcasted_iota(I32, (tm, tf), 0)
    h1 = pltpu.roll(hg, 1, axis=0)
    h2 = pltpu.roll(hg, 2, axis=0)
    if streaming:
        @pl.when((pl.program_id(0) * tm) % seq == 0)
        def _():
            carry_ref[f] = jnp.zeros((8, tf), F32)

        prev = carry_ref[f]
        c0 = prev[6:7, :]
        c1 = prev[7:8, :]
        h1 = jnp.where(row == 0, c1, h1)
        h2 = jnp.where(row == 0, c0, jnp.where(row == 1, c1, h2))
        carry_ref[f] = hg[tm - 8:, :]
        cs_ref[0] = hg[tm - 8:, :]
    else:
        pos = row % seq
        h1 = jnp.where(pos == 0, p1_ref[...], h1)
        h2 = jnp.where(pos < 2, p2_ref[...], h2)
        cs_ref[...] = hg
    cw = cw_ref[...]
    conv = cb_ref[...] + cw[0:1, :] * h2 + cw[1:2, :] * h1 + cw[2:3, :] * hg
    act = jax.nn.gelu(conv) * hu
    acc_ref[...] += _dot(act.astype(BF16), wd_ref[...])

    @pl.when(f == nf - 1)
    def _():
        y_ref[...] = _layer_norm(alpha * x1_ref[...] + acc_ref[...], g2_ref[...], b2_ref[...])


def _ffn(x2d, oa, ob, oc, lw, tm, tf, seq, prev=None):
    t = x2d.shape[0]
    nf = D_FF // tf
    streaming = prev is None
    alpha = lw["alpha"]
    row = lambda i, f: (i, 0)
    const = lambda i, f: (0, 0)
    in_specs = [pl.BlockSpec((tm, D_MODEL), row),
                pl.BlockSpec((tm, A_WIDTH), row),
                pl.BlockSpec((tm, B_WIDTH), row),
                pl.BlockSpec((tm, C_WIDTH), row),
                pl.BlockSpec((D_MODEL, D_MODEL), const),
                pl.BlockSpec((1, D_MODEL), const),
                pl.BlockSpec((1, D_MODEL), const),
                pl.BlockSpec((D_MODEL, tf), lambda i, f: (0, f)),
                pl.BlockSpec((D_MODEL, tf), lambda i, f: (0, f)),
                pl.BlockSpec((CONV_W, tf), lambda i, f: (0, f)),
                pl.BlockSpec((1, tf), lambda i, f: (0, f)),
                pl.BlockSpec((tf, D_MODEL), lambda i, f: (f, 0)),
                pl.BlockSpec((1, D_MODEL), const),
                pl.BlockSpec((1, D_MODEL), const)]
    args = [x2d, oa, ob, oc, lw["w_out"], lw["ln1_g"], lw["ln1_b"], lw["w_gate"], lw["w_up"],
            lw["conv_w"], lw["conv_b"], lw["w_down"], lw["ln2_g"], lw["ln2_b"]]
    scratch = [pltpu.VMEM((tm, D_MODEL), F32), pltpu.VMEM((tm, D_MODEL), BF16),
               pltpu.VMEM((tm, D_MODEL), F32)]
    if streaming:
        cs_spec = pl.BlockSpec((1, 8, tf), lambda i, f: (i, 0, f))
        cs_shape = jax.ShapeDtypeStruct((t // tm, 8, D_FF), F32)
        scratch.append(pltpu.VMEM((nf, 8, tf), F32))
    else:
        in_specs += [pl.BlockSpec((tm, tf), lambda i, f: (i, f))] * 2
        args += list(prev)
        cs_spec = pl.BlockSpec((tm, tf), lambda i, f: (i, f))
        cs_shape = jax.ShapeDtypeStruct((t, D_FF), F32)
    return pl.pallas_call(
        functools.partial(_ffn_kernel, tm=tm, alpha=alpha, streaming=streaming, seq=seq),
        grid=(t // tm, nf),
        in_specs=in_specs,
        out_specs=[pl.BlockSpec((tm, D_MODEL), row), cs_spec],
        out_shape=[jax.ShapeDtypeStruct((t, D_MODEL), F32), cs_shape],
        scratch_shapes=scratch,
        compiler_params=_params(2),
        name="ffn",
    )(*args)


def _pad_cols(w, width):
    return jnp.pad(w, ((0, 0), (0, width - w.shape[1])))


def _proj_weight(w_in):
    sizes = (A_WIDTH, A_WIDTH, B_WIDTH, HEAD_DIM, HEAD_DIM, IDX_HEADS * IDX_DIM, IDX_DIM, IDX_HEADS,
             C_WIDTH, C_WIDTH, C_WIDTH, C_WIDTH)
    splits = np.cumsum(sizes)[:-1].tolist()
    ua, va, qb, kb, vb, qib, kib, wib, qc, kc, vc, gc = jnp.split(w_in, splits, axis=1)
    qb = qb * (HEAD_DIM ** -0.5)
    cols = [ua, va]
    cols += [_pad_cols(qb[:, h * HEAD_DIM:(h + 1) * HEAD_DIM], LANE) for h in range(B_HEADS)]
    cols += [qc, kc, vc, gc]
    cols += [_pad_cols(qib[:, h * IDX_DIM:(h + 1) * IDX_DIM], LANE) for h in range(IDX_HEADS)]
    cols += [kb, vb, _pad_cols(jnp.concatenate([kib, wib], axis=1), LANE)]
    return jnp.concatenate(cols, axis=1).astype(BF16)


def _group_matrix(width):
    g = np.arange(width) // HEAD_DIM
    return jnp.asarray((g[:, None] == g[None, :]).astype(np.float32) / HEAD_DIM, BF16)


def _rotary_tables(pos):
    half = HEAD_DIM // 2
    freqs = ROPE_BASE ** (-jnp.arange(half, dtype=F32) / half)
    ang = pos.astype(F32)[:, None] * freqs
    cos, sin = jnp.cos(ang), jnp.sin(ang)
    cos = jnp.concatenate([cos, cos, cos, cos], axis=1)
    sin = jnp.concatenate([-sin, sin, -sin, sin], axis=1)
    return cos, sin


def _retention_tables(c):
    log_g = jnp.log(1.0 - 2.0 ** (-5.0 - jnp.arange(C_HEADS, dtype=F32)))
    i = jnp.arange(c, dtype=F32)
    diff = i[:, None] - i[None, :]
    dec = jnp.where(diff >= 0, jnp.exp(jnp.maximum(diff, 0.0)[None] * log_g[:, None, None]), 0.0)
    qd = jnp.repeat(jnp.exp((i + 1)[None] * log_g[:, None]).T, HEAD_DIM, axis=1)
    kd = jnp.repeat(jnp.exp((c - 1 - i)[None] * log_g[:, None]).T, HEAD_DIM, axis=1)
    cdl = jnp.repeat(jnp.exp(c * log_g), HEAD_DIM).reshape(C_HEADS // 2, LANE)
    cd = jnp.broadcast_to(cdl[:, :, None], (C_HEADS // 2, LANE, LANE))
    blk = np.arange(LANE) // HEAD_DIM
    bd = jnp.asarray((blk[:, None] == blk[None, :]).astype(np.float32))
    return dec, qd, kd, cd, bd


def _pair_states(r):
    n = r.shape[0]
    r = r.reshape(n, C_HEADS // 2, 2, HEAD_DIM, HEAD_DIM).astype(F32)
    z = jnp.zeros_like(r[:, :, 0])
    top = jnp.concatenate([r[:, :, 0], z], axis=-1)
    bot = jnp.concatenate([z, r[:, :, 1]], axis=-1)
    return jnp.concatenate([top, bot], axis=-2)


def _unpair_states(s):
    a = s[:, :, :HEAD_DIM, :HEAD_DIM]
    b = s[:, :, HEAD_DIM:, HEAD_DIM:]
    n = s.shape[0]
    return jnp.stack([a, b], axis=2).reshape(n, C_HEADS, HEAD_DIM, HEAD_DIM)


def _mixa_tables(ws, bs, c):
    tril = jnp.tril(jnp.ones((c, c), ws.dtype))
    w = ws[:, :c, :c] * tril
    wcat = jnp.concatenate([w[g] for g in range(A_GROUPS)], axis=1).astype(BF16)
    bias = jnp.repeat(bs[:, :c].T, HEAD_DIM, axis=1)
    return wcat, bias


def _layer_weights(l, depth, w_out, ln1_g, ln1_b, w_gate, w_up, conv_w, conv_b, w_down, ln2_g, ln2_b):
    return dict(alpha=float((2 * depth) ** 0.25),
                w_out=w_out[l].astype(BF16), ln1_g=ln1_g[l][None], ln1_b=ln1_b[l][None],
                w_gate=w_gate[l].astype(BF16), w_up=w_up[l].astype(BF16), conv_w=conv_w[l],
                conv_b=conv_b[l][None], w_down=w_down[l].astype(BF16),
                ln2_g=ln2_g[l][None], ln2_b=ln2_b[l][None])


def kernel(x_prompt, x_sample, cache_b_k, cache_b_v, cache_b_kidx, state_ret, state_ffn_conv,
           w_in, a_ln_g, a_ln_b, a_ws, a_bs, c_gn_g, w_out, ln1_g, ln1_b,
           w_gate, w_up, conv_w, conv_b, w_down, ln2_g, ln2_b):
    depth = w_in.shape[0]
    bp, s, _ = x_prompt.shape
    bs_, t, _ = x_sample.shape
    past = cache_b_k.shape[2]
    l_keys = past + t
    ts = bs_ * t
    tm_p = 512
    tm_s = ts if ts <= 512 else 512
    tf = 256
    assert s % tm_p == 0 and s % A_CHUNK == 0 and ts % tm_s == 0 and tm_s % t == 0 and t >= 2
    assert s % KEY_SUPER == 0 and KEY_SUPER % QUERY_BLOCK == 0
    nk_s = -(-l_keys // KEY_SUPER) * KEY_SUPER

    gmat_a = _group_matrix(A_WIDTH)
    gmat_c = _group_matrix(C_WIDTH)
    tri = jnp.asarray(np.triu(np.ones((KEY_SUPER, KEY_SUPER), np.float32), 1), BF16)
    cos_p, sin_p = _rotary_tables(jnp.arange(s))
    cos_s, sin_s = _rotary_tables(past + jnp.arange(t))
    ret_p = _retention_tables(CHUNK)
    ret_s = _retention_tables(t)

    xp = x_prompt.reshape(bp * s, D_MODEL)
    xs = x_sample.reshape(ts, D_MODEL)
    outs = {k: [] for k in ("kp", "vp", "kip", "rp", "cp", "ks", "vs", "kis", "rs", "cs", "avs")}
    for l in range(depth):
        w_proj = _proj_weight(w_in[l])
        lw = _layer_weights(l, depth, w_out, ln1_g, ln1_b, w_gate, w_up, conv_w, conv_b, w_down, ln2_g, ln2_b)
        ln_g, ln_b, gn_g = a_ln_g[l][None], a_ln_b[l][None], c_gn_g[l][None]

        pa, qb, pr, qi, kv, kw, kvb, kwb = _proj(xp, w_proj, tm_p)
        wcat, bias = _mixa_tables(a_ws[l], a_bs[l], A_CHUNK)
        oa, _ = _mixa(pa, wcat, bias, ln_g, ln_b, gmat_a, A_CHUNK, 512)
        r0 = jnp.zeros((bp, C_HEADS // 2, LANE, LANE), F32)
        oc, r_p = _ret(pr.reshape(bp, s, SEG_RET), cos_p, sin_p, *ret_p, gn_g, gmat_c, r0, CHUNK, 512)
        ob = _dsa(qb.reshape(bp, s, SEG_QB), qi.reshape(bp, s, SEG_QI), kw.reshape(bp, s, SEG_KW),
                  kvb.reshape(bp, s, SEG_KV), kwb.reshape(bp, s, SEG_KW), tri, QUERY_BLOCK, True, s)
        xp, c_p = _ffn(xp, oa, ob.reshape(bp * s, B_WIDTH), oc.reshape(bp * s, C_WIDTH), lw, tm_p, tf, s)
        kv3 = kv.reshape(bp, s, SEG_KV)
        outs["kp"].append(kv3[..., :HEAD_DIM])
        outs["vp"].append(kv3[..., HEAD_DIM:])
        outs["kip"].append(kw.reshape(bp, s, SEG_KW)[..., :IDX_DIM])
        outs["rp"].append(_unpair_states(r_p))
        outs["cp"].append(c_p[s // tm_p - 1::s // tm_p, 8 - (CONV_W - 1):, :])

        pa, qb, pr, qi, kv, kw, kvb, kwb = _proj(xs, w_proj, tm_s)
        wcat, bias = _mixa_tables(a_ws[l], a_bs[l], t)
        oa, av = _mixa(pa, wcat, bias, ln_g, ln_b, gmat_a, t, tm_s)
        r0 = _pair_states(state_ret[l])
        rows_s = t
        oc, r_s = _ret(pr.reshape(bs_, t, SEG_RET), cos_s, sin_s, *ret_s, gn_g, gmat_c, r0, t, rows_s)
        pad = nk_s - l_keys
        kv_full = jnp.concatenate(
            [jnp.concatenate([cache_b_k[l], cache_b_v[l]], axis=-1).astype(BF16),
             kvb.reshape(bs_, t, SEG_KV), jnp.zeros((bs_, pad, SEG_KV), BF16)], axis=1)
        kx_full = jnp.concatenate(
            [jnp.pad(cache_b_kidx[l], ((0, 0), (0, 0), (0, SEG_KW - IDX_DIM))).astype(BF16),
             kwb.reshape(bs_, t, SEG_KW), jnp.zeros((bs_, pad, SEG_KW), BF16)], axis=1)
        ob = _dsa(qb.reshape(bs_, t, SEG_QB), qi.reshape(bs_, t, SEG_QI), kw.reshape(bs_, t, SEG_KW),
                  kv_full, kx_full, tri, t, False, l_keys)
        cprev = state_ffn_conv[l]
        zeros = jnp.zeros((bs_, t - 2, D_FF), F32)
        p1 = jnp.concatenate([cprev[:, 1:2], zeros, zeros[:, :1]], axis=1).reshape(ts, D_FF)
        p2 = jnp.concatenate([cprev, zeros], axis=1).reshape(ts, D_FF)
        xs, hg_s = _ffn(xs, oa, ob.reshape(ts, B_WIDTH), oc.reshape(ts, C_WIDTH), lw, tm_s, tf, t, (p1, p2))
        kv3 = kv.reshape(bs_, t, SEG_KV)
        outs["ks"].append(kv3[..., :HEAD_DIM])
        outs["vs"].append(kv3[..., HEAD_DIM:])
        outs["kis"].append(kw.reshape(bs_, t, SEG_KW)[..., :IDX_DIM])
        outs["rs"].append(_unpair_states(r_s))
        outs["cs"].append(hg_s.reshape(bs_, t, D_FF)[:, t - (CONV_W - 1):, :])
        outs["avs"].append(av.reshape(bs_, t, A_WIDTH))

    st = lambda k: jnp.stack(outs[k])
    return (xp.reshape(bp, s, D_MODEL), xs.reshape(bs_, t, D_MODEL),
            st("kp"), st("vp"), st("kip"), st("rp"), st("cp"),
            st("ks"), st("vs"), st("kis"), st("rs"), st("cs"), st("avs"))
```

```python
import functools

import numpy as np
import jax
import jax.numpy as jnp
from jax import lax
from jax.experimental import pallas as pl
from jax.experimental.pallas import tpu as pltpu

F32 = jnp.float32
BF16 = jnp.bfloat16
I32 = jnp.int32

D_MODEL = 1024
CHUNK = 64
HEAD_DIM = 64
A_GROUPS = 4
C_HEADS = 6
B_HEADS = 6
A_WIDTH = A_GROUPS * HEAD_DIM
B_WIDTH = B_HEADS * HEAD_DIM
C_WIDTH = C_HEADS * HEAD_DIM
A_CHUNK = 128
IDX_HEADS = 8
IDX_DIM = 32
TOPK_MAX = 256
ROPE_BASE = 10000.0
D_FF = 2816
CONV_W = 3
LN_EPS = 1e-5

LANE = 128
QUERY_BLOCK = 256
KEY_SUPER = 4 * LANE
SELECT_VALUE_PASSES = 24
SELECT_MAX_PASSES = 64
VMEM_LIMIT = 48 * 1024 * 1024
FFN_VMEM_LIMIT = 58 * 1024 * 1024

SEG_A = 2 * A_WIDTH
SEG_QB = B_HEADS * LANE
SEG_RET = 4 * C_WIDTH
SEG_QI = IDX_HEADS * LANE
SEG_KV = LANE
SEG_KW = LANE
NEG_INF_KEY = np.int32(np.uint32(0x807FFFFF).astype(np.int64) - (1 << 32))
INT_MIN = np.int32(-(2 ** 31))
MIN_NORMAL_KEY = np.int32(0x00800000)


def _dot(a, b):
    return jnp.dot(a, b, preferred_element_type=F32)


def _dot_nt(a, b):
    return lax.dot_general(a, b, (((1,), (1,)), ((), ())), preferred_element_type=F32)


def _dot_tn(a, b):
    return lax.dot_general(a, b, (((0,), (0,)), ((), ())), preferred_element_type=F32)


def _split_dot(x, m):
    hi = x.astype(BF16)
    lo = (x - hi.astype(F32)).astype(BF16)
    return _dot(hi, m) + _dot(lo, m)


def _group_norm(x, gmat):
    mu = _split_dot(x, gmat)
    d = x - mu
    var = _split_dot(d * d, gmat)
    return d * lax.rsqrt(var + LN_EPS)


def _layer_norm(x, g, b):
    mu = jnp.mean(x, axis=-1, keepdims=True)
    d = x - mu
    var = jnp.mean(d * d, axis=-1, keepdims=True)
    return d * lax.rsqrt(var + LN_EPS) * g + b


def _params(n_axes, vmem_limit=VMEM_LIMIT):
    return pltpu.CompilerParams(dimension_semantics=("arbitrary",) * n_axes,
                                vmem_limit_bytes=vmem_limit)


def _proj_kernel(x_ref, w_ref, a_ref, qb_ref, ret_ref, qi_ref, kv_ref, kw_ref, kvb_ref, kwb_ref):
    xb = x_ref[...].astype(BF16)
    o = 0
    a_ref[...] = _dot(xb, w_ref[:, o:o + SEG_A]); o += SEG_A
    qb_ref[...] = _dot(xb, w_ref[:, o:o + SEG_QB]).astype(BF16); o += SEG_QB
    ret_ref[...] = _dot(xb, w_ref[:, o:o + SEG_RET]); o += SEG_RET
    qi_ref[...] = _dot(xb, w_ref[:, o:o + SEG_QI]).astype(BF16); o += SEG_QI
    kv = _dot(xb, w_ref[:, o:o + SEG_KV]); o += SEG_KV
    kw = _dot(xb, w_ref[:, o:o + SEG_KW])
    kv_ref[...] = kv
    kw_ref[...] = kw
    kvb_ref[...] = kv.astype(BF16)
    kwb_ref[...] = kw.astype(BF16)


def _proj(x2d, w_proj, tm):
    t = x2d.shape[0]
    n_cols = w_proj.shape[1]
    widths = (SEG_A, SEG_QB, SEG_RET, SEG_QI, SEG_KV, SEG_KW, SEG_KV, SEG_KW)
    dtypes = (F32, BF16, F32, BF16, F32, F32, BF16, BF16)
    return pl.pallas_call(
        _proj_kernel,
        grid=(t // tm,),
        in_specs=[pl.BlockSpec((tm, D_MODEL), lambda i: (i, 0)),
                  pl.BlockSpec((D_MODEL, n_cols), lambda i: (0, 0))],
        out_specs=[pl.BlockSpec((tm, w), lambda i: (i, 0)) for w in widths],
        out_shape=[jax.ShapeDtypeStruct((t, w), d) for w, d in zip(widths, dtypes)],
        compiler_params=_params(1),
        name="proj",
    )(x2d, w_proj)


def _mixa_kernel(pa_ref, wcat_ref, bias_ref, g_ref, b_ref, gmat_ref, oa_ref, av_ref, *, chunk, rows):
    pa = pa_ref[...]
    u = jax.nn.gelu(pa[:, :A_WIDTH])
    v = jax.nn.gelu(pa[:, A_WIDTH:])
    vn = _group_norm(v, gmat_ref[...]) * g_ref[...] + b_ref[...]
    av_ref[...] = vn
    group = lax.broadcasted_iota(I32, (chunk, A_WIDTH), 1) // HEAD_DIM
    wcat = wcat_ref[...]
    bias = bias_ref[...]
    for c in range(rows // chunk):
        vc = vn[c * chunk:(c + 1) * chunk, :]
        stacked = jnp.concatenate(
            [jnp.where(group == g, vc, 0.0).astype(BF16) for g in range(A_GROUPS)], axis=0)
        s = _dot(wcat, stacked) + bias
        oa_ref[c * chunk:(c + 1) * chunk, :] = u[c * chunk:(c + 1) * chunk, :] * s


def _mixa(pa, wcat, bias, ln_g, ln_b, gmat, chunk, rows):
    t = pa.shape[0]
    const = lambda i: (0, 0)
    return pl.pallas_call(
        functools.partial(_mixa_kernel, chunk=chunk, rows=rows),
        grid=(t // rows,),
        in_specs=[pl.BlockSpec((rows, SEG_A), lambda i: (i, 0)),
                  pl.BlockSpec(wcat.shape, const),
                  pl.BlockSpec(bias.shape, const),
                  pl.BlockSpec((1, A_WIDTH), const),
                  pl.BlockSpec((1, A_WIDTH), const),
                  pl.BlockSpec(gmat.shape, const)],
        out_specs=[pl.BlockSpec((rows, A_WIDTH), lambda i: (i, 0)),
                   pl.BlockSpec((rows, A_WIDTH), lambda i: (i, 0))],
        out_shape=[jax.ShapeDtypeStruct((t, A_WIDTH), F32),
                   jax.ShapeDtypeStruct((t, A_WIDTH), F32)],
        compiler_params=_params(1),
        name="mixa",
    )(pa, wcat, bias, ln_g, ln_b, gmat)


def _swap_halves(x):
    n = x.shape[1]
    lane = lax.broadcasted_iota(I32, x.shape, 1)
    fwd = pltpu.roll(x, n - HEAD_DIM // 2, axis=1)
    bwd = pltpu.roll(x, HEAD_DIM // 2, axis=1)
    return jnp.where(lane % HEAD_DIM < HEAD_DIM // 2, fwd, bwd)


def _ret_kernel(pr_ref, cos_ref, sin_ref, dec_ref, qd_ref, kd_ref, cd_ref, bd_ref, gng_ref, gmat_ref,
                r0_ref, oc_ref, rout_ref, state_ref, *, chunk, rows):
    @pl.when(pl.program_id(1) == 0)
    def _():
        state_ref[...] = r0_ref[0]

    lane = lax.broadcasted_iota(I32, (chunk, LANE), 1)
    first = (lane < HEAD_DIM).astype(F32)
    second = 1.0 - first
    qd = qd_ref[...]
    kd = kd_ref[...]
    ys = []
    for c in range(rows // chunk):
        sl = slice(c * chunk, (c + 1) * chunk)
        blk = pr_ref[0, sl, :]
        cos = jnp.concatenate([cos_ref[sl, :]] * 3, axis=1)
        sin = jnp.concatenate([sin_ref[sl, :]] * 3, axis=1)
        q = blk[:, 0:C_WIDTH]
        k = blk[:, C_WIDTH:2 * C_WIDTH]
        v = blk[:, 2 * C_WIDTH:3 * C_WIDTH]
        q = q * cos + _swap_halves(q) * sin
        k = (k * cos + _swap_halves(k) * sin) * (HEAD_DIM ** -0.5)
        kdec = k * kd
        pairs = []
        for p in range(C_HEADS // 2):
            ls = slice(p * LANE, (p + 1) * LANE)
            qp, kp, vp = q[:, ls], k[:, ls], v[:, ls]
            kpb = kp.astype(BF16)
            intra = jnp.zeros((chunk, LANE), F32)
            for half, m in ((0, first), (1, second)):
                s = _dot_nt((qp * m).astype(BF16), kpb) * dec_ref[2 * p + half]
                intra = intra + _dot(s.astype(BF16), (vp * m).astype(BF16))
            st = state_ref[p]
            cross = _dot(qp.astype(BF16), st.astype(BF16)) * qd[:, ls]
            kv = _dot_tn(kdec[:, ls].astype(BF16), vp.astype(BF16))
            state_ref[p] = cd_ref[p] * st + kv * bd_ref[...]
            pairs.append(intra + cross)
        ys.append(jnp.concatenate(pairs, axis=1))
    y = jnp.concatenate(ys, axis=0)
    yn = _group_norm(y, gmat_ref[...]) * gng_ref[...]
    g = pr_ref[0, :, 3 * C_WIDTH:4 * C_WIDTH]
    oc_ref[0] = jax.nn.silu(g) * yn
    rout_ref[0] = state_ref[...]


def _ret(pr, cos, sin, dec, qd, kd, cd, bd, gn_g, gmat, r0, chunk, rows):
    nb, t, _ = pr.shape
    npair = C_HEADS // 2
    c2 = lambda b, i: (0, 0)
    c3 = lambda b, i: (0, 0, 0)
    return pl.pallas_call(
        functools.partial(_ret_kernel, chunk=chunk, rows=rows),
        grid=(nb, t // rows),
        in_specs=[pl.BlockSpec((1, rows, SEG_RET), lambda b, i: (b, i, 0)),
                  pl.BlockSpec((rows, LANE), lambda b, i: (i, 0)),
                  pl.BlockSpec((rows, LANE), lambda b, i: (i, 0)),
                  pl.BlockSpec(dec.shape, c3),
                  pl.BlockSpec(qd.shape, c2),
                  pl.BlockSpec(kd.shape, c2),
                  pl.BlockSpec(cd.shape, c3),
                  pl.BlockSpec(bd.shape, c2),
                  pl.BlockSpec((1, C_WIDTH), c2),
                  pl.BlockSpec(gmat.shape, c2),
                  pl.BlockSpec((1, npair, LANE, LANE), lambda b, i: (b, 0, 0, 0))],
        out_specs=[pl.BlockSpec((1, rows, C_WIDTH), lambda b, i: (b, i, 0)),
                   pl.BlockSpec((1, npair, LANE, LANE), lambda b, i: (b, 0, 0, 0))],
        out_shape=[jax.ShapeDtypeStruct((nb, t, C_WIDTH), F32),
                   jax.ShapeDtypeStruct((nb, npair, LANE, LANE), F32)],
        scratch_shapes=[pltpu.VMEM((npair, LANE, LANE), F32)],
        compiler_params=_params(2),
        name="ret",
    )(pr, cos, sin, dec, qd, kd, cd, bd, gn_g, gmat, r0)


def _lane_blocks(x):
    return [x[:, i * LANE:(i + 1) * LANE] for i in range(x.shape[1] // LANE)]


def _key_to_float(key):
    return pltpu.bitcast(key ^ ((key >> 31) & 0x7FFFFFFF), F32)


def _threshold_float(key):
    subnormal = jnp.logical_and(key > 0, key < MIN_NORMAL_KEY)
    return _key_to_float(jnp.where(subnormal, MIN_NORMAL_KEY, key))


def _float_to_key(x):
    bits = pltpu.bitcast(x, I32)
    return bits ^ ((bits >> 31) & 0x7FFFFFFF)


def _dsa_kernel(qb_ref, qi_ref, kw_ref, kv_ref, kx_ref, tri_ref, o_ref,
                sc_ref, wb_ref, m_ref, l_ref, acc_ref, mx_ref, mn_ref,
                *, tq, nsb_total, causal, k_sel, n_keys):
    ks = KEY_SUPER
    nrep = ks // LANE
    if causal:
        j = pl.program_id(1)
        nsb = (j * tq) // ks + 1
        row = lax.broadcasted_iota(I32, (tq, 1), 0)
        limit = ((j * tq + row) // CHUNK + 1) * CHUNK
    else:
        nsb = nsb_total
        limit = jnp.full((tq, 1), n_keys, I32)
    lane = lax.broadcasted_iota(I32, (tq, ks), 1)

    def wide(x):
        return jnp.concatenate([x] * nrep, axis=1)

    qs = jnp.concatenate([qb_ref[0][:, h * LANE:(h + 1) * LANE] for h in range(B_HEADS)], axis=0)
    qis = jnp.concatenate([qi_ref[0][:, h * LANE:(h + 1) * LANE] for h in range(IDX_HEADS)], axis=0)

    w = kw_ref[0][:, IDX_DIM:IDX_DIM + IDX_HEADS] * (IDX_DIM ** -0.5 * IDX_HEADS ** -0.5)
    for h in range(IDX_HEADS):
        wb_ref[h] = jnp.broadcast_to(w[:, h:h + 1], (tq, LANE))

    mx_ref[...] = jnp.full(mx_ref.shape, -jnp.inf, F32)
    mn_ref[...] = jnp.full(mn_ref.shape, jnp.inf, F32)

    def score_body(sb, carry):
        off = pl.multiple_of(sb * ks, ks)
        d = _dot_nt(qis, kx_ref[0, pl.ds(off, ks), :])
        acc = jnp.zeros((tq, ks), F32)
        for h in range(IDX_HEADS):
            acc = acc + jnp.maximum(d[h * tq:(h + 1) * tq], 0.0) * wide(wb_ref[h])
        adm = off + lane < limit
        sc = jnp.where(adm, acc, -jnp.inf)
        sc_ref[sb] = sc
        mx_ref[...] = functools.reduce(jnp.maximum, _lane_blocks(sc), mx_ref[...])
        mn_ref[...] = functools.reduce(jnp.minimum, _lane_blocks(jnp.where(adm, acc, jnp.inf)), mn_ref[...])
        return carry

    lax.fori_loop(0, nsb, score_body, 0)

    rows = min(tq, LANE)
    dense = tq % LANE == 0

    def to_state(per_row_lanes, reduce_fn):
        if dense:
            return reduce_fn(per_row_lanes.T, axis=0, keepdims=True)
        return reduce_fn(per_row_lanes, axis=1, keepdims=True)

    def to_rows(state, r0):
        if dense:
            return jnp.broadcast_to(state[:, r0:r0 + rows], (rows, LANE)).T
        return jnp.broadcast_to(state, (rows, LANE))

    def gather_state(parts):
        return jnp.concatenate(parts, axis=1 if dense else 0) if len(parts) > 1 else parts[0]

    def count(cand, strict):
        starts = range(0, tq, rows)
        cbs = [to_rows(cand, r0) for r0 in starts]
        accs = []
        for r0, cb in zip(starts, cbs):
            def body(sb, acc, r0=r0, cb=cb):
                for i in range(nrep):
                    x = sc_ref[sb, r0:r0 + rows, i * LANE:(i + 1) * LANE]
                    acc = acc + jnp.where(x > cb if strict else x >= cb, 1.0, 0.0)
                return acc

            accs.append(lax.fori_loop(0, nsb, body, jnp.zeros((rows, LANE), F32)))
        return gather_state([to_state(acc, jnp.sum) for acc in accs])

    if causal and dense:
        pos = j * tq + lax.broadcasted_iota(I32, (1, tq), 1)
        n_adm = ((pos // CHUNK + 1) * CHUNK).astype(F32)
    else:
        n_adm = jnp.minimum(limit, n_keys).astype(F32)
    row_min = gather_state([to_state(mn_ref[r0:r0 + rows], jnp.min) for r0 in range(0, tq, rows)])
    row_max = gather_state([to_state(mx_ref[r0:r0 + rows], jnp.max) for r0 in range(0, tq, rows)])
    short = n_adm <= k_sel
    lo0 = jnp.where(short, NEG_INF_KEY + 1, _float_to_key(row_min))
    hi0 = jnp.where(short, NEG_INF_KEY + 2, _float_to_key(row_max) + 1)
    c0 = jnp.where(short, float(k_sel), n_adm)

    def unresolved(lo, hi, c_lo):
        return jnp.logical_and(hi > lo + 1, c_lo != k_sel)

    def any_unresolved(lo, hi, c_lo):
        return jnp.max(jnp.where(unresolved(lo, hi, c_lo), 1.0, 0.0))

    def value_mid(lo, hi):
        return _float_to_key(0.5 * _key_to_float(lo) + 0.5 * _key_to_float(hi))

    def step(lo, hi, c_lo, cand, strict):
        live = unresolved(lo, hi, c_lo)
        cand = jnp.minimum(jnp.maximum(cand, lo + 1 - strict), hi - 1 - strict)
        cnt = count(_threshold_float(cand), bool(strict))
        edge = cand + strict
        take = jnp.logical_and(live, cnt >= k_sel)
        drop = jnp.logical_and(live, cnt < k_sel)
        return jnp.where(take, edge, lo), jnp.where(drop, edge, hi), jnp.where(take, cnt, c_lo)

    lo, hi, c_lo = step(lo0, hi0, c0, jnp.where(jnp.logical_and(lo0 < 0, hi0 > 0), 0, value_mid(lo0, hi0)), 0)
    lo, hi, c_lo = step(lo, hi, c_lo, jnp.where(lo == 0, 0, value_mid(lo, hi)), 1)

    def sel_cond(carry):
        return jnp.logical_and(carry[0] < SELECT_MAX_PASSES, carry[1] > 0.5)

    def sel_body(carry):
        it, _, lo, hi, c_lo = carry
        kmid = (lo >> 1) + (hi >> 1) + (lo & hi & 1)
        lo, hi, c_lo = step(lo, hi, c_lo, jnp.where(it < SELECT_VALUE_PASSES, value_mid(lo, hi), kmid), 0)
        return it + 1, any_unresolved(lo, hi, c_lo), lo, hi, c_lo

    _, _, lo, _, c_lo = lax.while_loop(
        sel_cond, sel_body, (jnp.int32(0), any_unresolved(lo, hi, c_lo), lo, hi, c_lo))
    thr = _threshold_float(lo)

    def rows_of(state):
        return jnp.concatenate([to_rows(state, r0) for r0 in range(0, tq, rows)], axis=0)

    thr_rows = wide(rows_of(thr))

    any_tied = jnp.max(jnp.where(c_lo > k_sel, 1.0, 0.0)) > 0.5

    @pl.when(any_tied)
    def _():
        need = rows_of(k_sel - count(thr, True))[:, :1]

        def body(sb, before):
            sc = sc_ref[sb]
            eq = sc == thr_rows
            eqf = jnp.where(eq, 1.0, 0.0)
            prefix = _dot(eqf.astype(BF16), tri_ref[...]) + before
            drop = jnp.logical_and(eq, prefix >= need)
            sc_ref[sb] = jnp.where(drop, -jnp.inf, sc)
            return before + jnp.sum(eqf, axis=1, keepdims=True)

        lax.fori_loop(0, nsb, body, jnp.zeros((tq, 1), F32))

    def bias_body(sb, carry):
        sc_ref[sb] = jnp.where(sc_ref[sb] >= thr_rows, 0.0, -jnp.inf)
        return carry

    lax.fori_loop(0, nsb, bias_body, 0)

    def logits(sb):
        off = pl.multiple_of(sb * ks, ks)
        kvb = kv_ref[0, pl.ds(off, ks), :]
        return _dot_nt(qs, kvb), kvb

    m_ref[...] = jnp.full(m_ref.shape, -jnp.inf, F32)

    def max_body(sb, carry):
        s, _ = logits(sb)
        bias = sc_ref[sb]
        for h in range(B_HEADS):
            blocks = _lane_blocks(s[h * tq:(h + 1) * tq] + bias)
            m_ref[h] = functools.reduce(jnp.maximum, blocks, m_ref[h])
        return carry

    lax.fori_loop(0, nsb, max_body, 0)
    for h in range(B_HEADS):
        m_ref[h] = jnp.broadcast_to(jnp.max(m_ref[h], axis=1, keepdims=True), (tq, LANE))
    l_ref[...] = jnp.zeros(l_ref.shape, F32)
    acc_ref[...] = jnp.zeros(acc_ref.shape, F32)

    def pv_body(sb, carry):
        s, kvb = logits(sb)
        bias = sc_ref[sb]
        ps = []
        for h in range(B_HEADS):
            p = jnp.exp(s[h * tq:(h + 1) * tq] + bias - wide(m_ref[h]))
            l_ref[h] += sum(_lane_blocks(p))
            ps.append(p.astype(BF16))
        acc_ref[...] += _dot(jnp.concatenate(ps, axis=0), kvb)
        return carry

    lax.fori_loop(0, nsb, pv_body, 0)

    outs = [acc_ref[h * tq:(h + 1) * tq] / jnp.sum(l_ref[h], axis=1, keepdims=True) for h in range(B_HEADS)]
    lane_head = lax.broadcasted_iota(I32, (tq, LANE), 1)
    for p in range(B_HEADS // 2):
        even = pltpu.roll(outs[2 * p], HEAD_DIM, axis=1)
        o_ref[0, :, p * LANE:(p + 1) * LANE] = jnp.where(lane_head < HEAD_DIM, even, outs[2 * p + 1])


def _dsa(qb, qi, kw, kvb, kxb, tri, tq, causal, n_keys):
    nb, t, _ = qb.shape
    nk = kvb.shape[1]
    nsb_total = nk // KEY_SUPER
    kernel = functools.partial(_dsa_kernel, tq=tq, nsb_total=nsb_total, causal=causal,
                               k_sel=min(TOPK_MAX, n_keys // 4), n_keys=n_keys)
    return pl.pallas_call(
        kernel,
        grid=(nb, t // tq),
        in_specs=[pl.BlockSpec((1, tq, SEG_QB), lambda b, j: (b, j, 0)),
                  pl.BlockSpec((1, tq, SEG_QI), lambda b, j: (b, j, 0)),
                  pl.BlockSpec((1, tq, SEG_KW), lambda b, j: (b, j, 0)),
                  pl.BlockSpec((1, nk, SEG_KV), lambda b, j: (b, 0, 0)),
                  pl.BlockSpec((1, nk, SEG_KW), lambda b, j: (b, 0, 0)),
                  pl.BlockSpec(tri.shape, lambda b, j: (0, 0))],
        out_specs=pl.BlockSpec((1, tq, B_WIDTH), lambda b, j: (b, j, 0)),
        out_shape=jax.ShapeDtypeStruct((nb, t, B_WIDTH), F32),
        scratch_shapes=[pltpu.VMEM((nsb_total, tq, KEY_SUPER), F32),
                        pltpu.VMEM((IDX_HEADS, tq, LANE), F32),
                        pltpu.VMEM((B_HEADS, tq, LANE), F32),
                        pltpu.VMEM((B_HEADS, tq, LANE), F32),
                        pltpu.VMEM((B_HEADS * tq, LANE), F32),
                        pltpu.VMEM((tq, LANE), F32),
                        pltpu.VMEM((tq, LANE), F32)],
        compiler_params=_params(2),
        name="dsa",
    )(qb, qi, kw, kvb, kxb, tri)


def _ffn_kernel(*refs, tm, alpha, streaming, seq):
    if streaming:
        (x_ref, oa_ref, ob_ref, oc_ref, wo_ref, g1_ref, b1_ref, wg_ref, wu_ref, cw_ref, cb_ref, wd_ref,
         g2_ref, b2_ref, y_ref, cs_ref, x1_ref, x1b_ref, acc_ref, carry_ref) = refs
    else:
        (x_ref, oa_ref, ob_ref, oc_ref, wo_ref, g1_ref, b1_ref, wg_ref, wu_ref, cw_ref, cb_ref, wd_ref,
         g2_ref, b2_ref, p1_ref, p2_ref, y_ref, cs_ref, x1_ref, x1b_ref, acc_ref) = refs
    f = pl.program_id(1)
    nf = pl.num_programs(1)

    @pl.when(f == 0)
    def _():
        mix = (_dot(oa_ref[...].astype(BF16), wo_ref[0:A_WIDTH, :])
               + _dot(ob_ref[...].astype(BF16), wo_ref[A_WIDTH:A_WIDTH + B_WIDTH, :])
               + _dot(oc_ref[...].astype(BF16), wo_ref[A_WIDTH + B_WIDTH:, :]))
        x1 = _layer_norm(alpha * x_ref[...] + mix, g1_ref[...], b1_ref[...])
        x1_ref[...] = x1
        x1b_ref[...] = x1.astype(BF16)
        acc_ref[...] = jnp.zeros_like(acc_ref)

    xb = x1b_ref[...]
    hg = _dot(xb, wg_ref[...])
    hu = _dot(xb, wu_ref[...])
    tf = hg.shape[1]
    h1 = pltpu.roll(hg, 1, axis=0)
    h2 = pltpu.roll(hg, 2, axis=0)
    if streaming:
        @pl.when((pl.program_id(0) * tm) % seq == 0)
        def _():
            carry_ref[f] = jnp.zeros((8, tf), F32)

        prev = carry_ref[f]
        c0 = prev[6:7, :]
        c1 = prev[7:8, :]
        top = lax.broadcasted_iota(I32, (8, tf), 0)
        h1 = jnp.concatenate([jnp.where(top == 0, c1, h1[:8]), h1[8:]], axis=0)
        h2 = jnp.concatenate([jnp.where(top == 0, c0, jnp.where(top == 1, c1, h2[:8])), h2[8:]], axis=0)
        carry_ref[f] = hg[tm - 8:, :]
        cs_ref[0] = hg[tm - 8:, :]
    else:
        pos = lax.broadcasted_iota(I32, (tm, tf), 0) % seq
        h1 = jnp.where(pos == 0, p1_ref[...], h1)
        h2 = jnp.where(pos < 2, p2_ref[...], h2)
        cs_ref[...] = hg
    cw = cw_ref[...]
    conv = cb_ref[...] + cw[0:1, :] * h2 + cw[1:2, :] * h1 + cw[2:3, :] * hg
    act = jax.nn.gelu(conv) * hu
    acc_ref[...] += _dot(act.astype(BF16), wd_ref[...])

    @pl.when(f == nf - 1)
    def _():
        y_ref[...] = _layer_norm(alpha * x1_ref[...] + acc_ref[...], g2_ref[...], b2_ref[...])


def _ffn(x2d, oa, ob, oc, lw, tm, tf, seq, prev=None):
    t = x2d.shape[0]
    nf = D_FF // tf
    streaming = prev is None
    alpha = lw["alpha"]
    row = lambda i, f: (i, 0)
    const = lambda i, f: (0, 0)
    in_specs = [pl.BlockSpec((tm, D_MODEL), row),
                pl.BlockSpec((tm, A_WIDTH), row),
                pl.BlockSpec((tm, B_WIDTH), row),
                pl.BlockSpec((tm, C_WIDTH), row),
                pl.BlockSpec((D_MODEL, D_MODEL), const),
                pl.BlockSpec((1, D_MODEL), const),
                pl.BlockSpec((1, D_MODEL), const),
                pl.BlockSpec((D_MODEL, tf), lambda i, f: (0, f)),
                pl.BlockSpec((D_MODEL, tf), lambda i, f: (0, f)),
                pl.BlockSpec((CONV_W, tf), lambda i, f: (0, f)),
                pl.BlockSpec((1, tf), lambda i, f: (0, f)),
                pl.BlockSpec((tf, D_MODEL), lambda i, f: (f, 0)),
                pl.BlockSpec((1, D_MODEL), const),
                pl.BlockSpec((1, D_MODEL), const)]
    args = [x2d, oa, ob, oc, lw["w_out"], lw["ln1_g"], lw["ln1_b"], lw["w_gate"], lw["w_up"],
            lw["conv_w"], lw["conv_b"], lw["w_down"], lw["ln2_g"], lw["ln2_b"]]
    scratch = [pltpu.VMEM((tm, D_MODEL), F32), pltpu.VMEM((tm, D_MODEL), BF16),
               pltpu.VMEM((tm, D_MODEL), F32)]
    if streaming:
        cs_spec = pl.BlockSpec((1, 8, tf), lambda i, f: (i, 0, f))
        cs_shape = jax.ShapeDtypeStruct((t // tm, 8, D_FF), F32)
        scratch.append(pltpu.VMEM((nf, 8, tf), F32))
    else:
        in_specs += [pl.BlockSpec((tm, tf), lambda i, f: (i, f))] * 2
        args += list(prev)
        cs_spec = pl.BlockSpec((tm, tf), lambda i, f: (i, f))
        cs_shape = jax.ShapeDtypeStruct((t, D_FF), F32)
    return pl.pallas_call(
        functools.partial(_ffn_kernel, tm=tm, alpha=alpha, streaming=streaming, seq=seq),
        grid=(t // tm, nf),
        in_specs=in_specs,
        out_specs=[pl.BlockSpec((tm, D_MODEL), row), cs_spec],
        out_shape=[jax.ShapeDtypeStruct((t, D_MODEL), F32), cs_shape],
        scratch_shapes=scratch,
        compiler_params=_params(2, FFN_VMEM_LIMIT),
        name="ffn",
    )(*args)


def _pad_cols(w, width):
    return jnp.pad(w, ((0, 0), (0, width - w.shape[1])))


def _proj_weight(w_in):
    sizes = (A_WIDTH, A_WIDTH, B_WIDTH, HEAD_DIM, HEAD_DIM, IDX_HEADS * IDX_DIM, IDX_DIM, IDX_HEADS,
             C_WIDTH, C_WIDTH, C_WIDTH, C_WIDTH)
    splits = np.cumsum(sizes)[:-1].tolist()
    ua, va, qb, kb, vb, qib, kib, wib, qc, kc, vc, gc = jnp.split(w_in, splits, axis=1)
    qb = qb * (HEAD_DIM ** -0.5)
    cols = [ua, va]
    cols += [_pad_cols(qb[:, h * HEAD_DIM:(h + 1) * HEAD_DIM], LANE) for h in range(B_HEADS)]
    cols += [qc, kc, vc, gc]
    cols += [_pad_cols(qib[:, h * IDX_DIM:(h + 1) * IDX_DIM], LANE) for h in range(IDX_HEADS)]
    cols += [kb, vb, _pad_cols(jnp.concatenate([kib, wib], axis=1), LANE)]
    return jnp.concatenate(cols, axis=1).astype(BF16)


def _group_matrix(width):
    g = np.arange(width) // HEAD_DIM
    return jnp.asarray((g[:, None] == g[None, :]).astype(np.float32) / HEAD_DIM, BF16)


def _rotary_tables(pos):
    half = HEAD_DIM // 2
    freqs = ROPE_BASE ** (-jnp.arange(half, dtype=F32) / half)
    ang = pos.astype(F32)[:, None] * freqs
    cos, sin = jnp.cos(ang), jnp.sin(ang)
    cos = jnp.concatenate([cos, cos, cos, cos], axis=1)
    sin = jnp.concatenate([-sin, sin, -sin, sin], axis=1)
    return cos, sin


def _retention_tables(c):
    log_g = jnp.log(1.0 - 2.0 ** (-5.0 - jnp.arange(C_HEADS, dtype=F32)))
    i = jnp.arange(c, dtype=F32)
    diff = i[:, None] - i[None, :]
    dec = jnp.where(diff >= 0, jnp.exp(jnp.maximum(diff, 0.0)[None] * log_g[:, None, None]), 0.0)
    qd = jnp.repeat(jnp.exp((i + 1)[None] * log_g[:, None]).T, HEAD_DIM, axis=1)
    kd = jnp.repeat(jnp.exp((c - 1 - i)[None] * log_g[:, None]).T, HEAD_DIM, axis=1)
    cdl = jnp.repeat(jnp.exp(c * log_g), HEAD_DIM).reshape(C_HEADS // 2, LANE)
    cd = jnp.broadcast_to(cdl[:, :, None], (C_HEADS // 2, LANE, LANE))
    blk = np.arange(LANE) // HEAD_DIM
    bd = jnp.asarray((blk[:, None] == blk[None, :]).astype(np.float32))
    return dec, qd, kd, cd, bd


def _pair_states(r):
    n = r.shape[0]
    r = r.reshape(n, C_HEADS // 2, 2, HEAD_DIM, HEAD_DIM).astype(F32)
    z = jnp.zeros_like(r[:, :, 0])
    top = jnp.concatenate([r[:, :, 0], z], axis=-1)
    bot = jnp.concatenate([z, r[:, :, 1]], axis=-1)
    return jnp.concatenate([top, bot], axis=-2)


def _unpair_states(s):
    a = s[:, :, :HEAD_DIM, :HEAD_DIM]
    b = s[:, :, HEAD_DIM:, HEAD_DIM:]
    n = s.shape[0]
    return jnp.stack([a, b], axis=2).reshape(n, C_HEADS, HEAD_DIM, HEAD_DIM)


def _mixa_tables(ws, bs, c):
    tril = jnp.tril(jnp.ones((c, c), ws.dtype))
    w = ws[:, :c, :c] * tril
    wcat = jnp.concatenate([w[g] for g in range(A_GROUPS)], axis=1).astype(BF16)
    bias = jnp.repeat(bs[:, :c].T, HEAD_DIM, axis=1)
    return wcat, bias


def _layer_weights(l, depth, w_out, ln1_g, ln1_b, w_gate, w_up, conv_w, conv_b, w_down, ln2_g, ln2_b):
    return dict(alpha=float((2 * depth) ** 0.25),
                w_out=w_out[l].astype(BF16), ln1_g=ln1_g[l][None], ln1_b=ln1_b[l][None],
                w_gate=w_gate[l].astype(BF16), w_up=w_up[l].astype(BF16), conv_w=conv_w[l],
                conv_b=conv_b[l][None], w_down=w_down[l].astype(BF16),
                ln2_g=ln2_g[l][None], ln2_b=ln2_b[l][None])


def kernel(x_prompt, x_sample, cache_b_k, cache_b_v, cache_b_kidx, state_ret, state_ffn_conv,
           w_in, a_ln_g, a_ln_b, a_ws, a_bs, c_gn_g, w_out, ln1_g, ln1_b,
           w_gate, w_up, conv_w, conv_b, w_down, ln2_g, ln2_b):
    depth = w_in.shape[0]
    bp, s, _ = x_prompt.shape
    bs_, t, _ = x_sample.shape
    past = cache_b_k.shape[2]
    l_keys = past + t
    ts = bs_ * t
    tm_p = 512
    tm_s = ts if ts <= 512 else 512
    tf_p, tf_s = D_FF // 2, 256
    assert s % tm_p == 0 and s % A_CHUNK == 0 and ts % tm_s == 0 and tm_s % t == 0 and t >= 2
    assert s % KEY_SUPER == 0 and KEY_SUPER % QUERY_BLOCK == 0
    nk_s = -(-l_keys // KEY_SUPER) * KEY_SUPER

    gmat_a = _group_matrix(A_WIDTH)
    gmat_c = _group_matrix(C_WIDTH)
    tri = jnp.asarray(np.triu(np.ones((KEY_SUPER, KEY_SUPER), np.float32), 1), BF16)
    cos_p, sin_p = _rotary_tables(jnp.arange(s))
    cos_s, sin_s = _rotary_tables(past + jnp.arange(t))
    ret_p = _retention_tables(CHUNK)
    ret_s = _retention_tables(t)

    xp = x_prompt.reshape(bp * s, D_MODEL)
    xs = x_sample.reshape(ts, D_MODEL)
    outs = {k: [] for k in ("kp", "vp", "kip", "rp", "cp", "ks", "vs", "kis", "rs", "cs", "avs")}
    for l in range(depth):
        w_proj = _proj_weight(w_in[l])
        lw = _layer_weights(l, depth, w_out, ln1_g, ln1_b, w_gate, w_up, conv_w, conv_b, w_down, ln2_g, ln2_b)
        ln_g, ln_b, gn_g = a_ln_g[l][None], a_ln_b[l][None], c_gn_g[l][None]

        pa, qb, pr, qi, kv, kw, kvb, kwb = _proj(xp, w_proj, tm_p)
        wcat, bias = _mixa_tables(a_ws[l], a_bs[l], A_CHUNK)
        oa, _ = _mixa(pa, wcat, bias, ln_g, ln_b, gmat_a, A_CHUNK, 512)
        r0 = jnp.zeros((bp, C_HEADS // 2, LANE, LANE), F32)
        oc, r_p = _ret(pr.reshape(bp, s, SEG_RET), cos_p, sin_p, *ret_p, gn_g, gmat_c, r0, CHUNK, 512)
        ob = _dsa(qb.reshape(bp, s, SEG_QB), qi.reshape(bp, s, SEG_QI), kw.reshape(bp, s, SEG_KW),
                  kvb.reshape(bp, s, SEG_KV), kwb.reshape(bp, s, SEG_KW), tri, QUERY_BLOCK, True, s)
        xp, c_p = _ffn(xp, oa, ob.reshape(bp * s, B_WIDTH), oc.reshape(bp * s, C_WIDTH), lw, tm_p, tf_p, s)
        kv3 = kv.reshape(bp, s, SEG_KV)
        outs["kp"].append(kv3[..., :HEAD_DIM])
        outs["vp"].append(kv3[..., HEAD_DIM:])
        outs["kip"].append(kw.reshape(bp, s, SEG_KW)[..., :IDX_DIM])
        outs["rp"].append(_unpair_states(r_p))
        outs["cp"].append(c_p[s // tm_p - 1::s // tm_p, 8 - (CONV_W - 1):, :])

        pa, qb, pr, qi, kv, kw, kvb, kwb = _proj(xs, w_proj, tm_s)
        wcat, bias = _mixa_tables(a_ws[l], a_bs[l], t)
        oa, av = _mixa(pa, wcat, bias, ln_g, ln_b, gmat_a, t, tm_s)
        r0 = _pair_states(state_ret[l])
        rows_s = t
        oc, r_s = _ret(pr.reshape(bs_, t, SEG_RET), cos_s, sin_s, *ret_s, gn_g, gmat_c, r0, t, rows_s)
        pad = nk_s - l_keys
        kv_full = jnp.concatenate(
            [jnp.concatenate([cache_b_k[l], cache_b_v[l]], axis=-1).astype(BF16),
             kvb.reshape(bs_, t, SEG_KV), jnp.zeros((bs_, pad, SEG_KV), BF16)], axis=1)
        kx_full = jnp.concatenate(
            [jnp.pad(cache_b_kidx[l], ((0, 0), (0, 0), (0, SEG_KW - IDX_DIM))).astype(BF16),
             kwb.reshape(bs_, t, SEG_KW), jnp.zeros((bs_, pad, SEG_KW), BF16)], axis=1)
        ob = _dsa(qb.reshape(bs_, t, SEG_QB), qi.reshape(bs_, t, SEG_QI), kw.reshape(bs_, t, SEG_KW),
                  kv_full, kx_full, tri, t, False, l_keys)
        cprev = state_ffn_conv[l]
        zeros = jnp.zeros((bs_, t - 2, D_FF), F32)
        p1 = jnp.concatenate([cprev[:, 1:2], zeros, zeros[:, :1]], axis=1).reshape(ts, D_FF)
        p2 = jnp.concatenate([cprev, zeros], axis=1).reshape(ts, D_FF)
        xs, hg_s = _ffn(xs, oa, ob.reshape(ts, B_WIDTH), oc.reshape(ts, C_WIDTH), lw, tm_s, tf_s, t, (p1, p2))
        kv3 = kv.reshape(bs_, t, SEG_KV)
        outs["ks"].append(kv3[..., :HEAD_DIM])
        outs["vs"].append(kv3[..., HEAD_DIM:])
        outs["kis"].append(kw.reshape(bs_, t, SEG_KW)[..., :IDX_DIM])
        outs["rs"].append(_unpair_states(r_s))
        outs["cs"].append(hg_s.reshape(bs_, t, D_FF)[:, t - (CONV_W - 1):, :])
        outs["avs"].append(av.reshape(bs_, t, A_WIDTH))

    st = lambda k: jnp.stack(outs[k])
    return (xp.reshape(bp, s, D_MODEL), xs.reshape(bs_, t, D_MODEL),
            st("kp"), st("vp"), st("kip"), st("rp"), st("cp"),
            st("ks"), st("vs"), st("kis"), st("rs"), st("cs"), st("avs"))
```

```python
import functools

import numpy as np
import jax
import jax.numpy as jnp
from jax import lax
from jax.experimental import pallas as pl
from jax.experimental.pallas import tpu as pltpu

F32 = jnp.float32
BF16 = jnp.bfloat16
I32 = jnp.int32

D_MODEL = 1024
CHUNK = 64
HEAD_DIM = 64
A_GROUPS = 4
C_HEADS = 6
B_HEADS = 6
A_WIDTH = A_GROUPS * HEAD_DIM
B_WIDTH = B_HEADS * HEAD_DIM
C_WIDTH = C_HEADS * HEAD_DIM
A_CHUNK = 128
IDX_HEADS = 8
IDX_DIM = 32
TOPK_MAX = 256
ROPE_BASE = 10000.0
D_FF = 2816
CONV_W = 3
LN_EPS = 1e-5

LANE = 128
QUERY_BLOCK = 512
KEY_SUPER = 4 * LANE
SELECT_VALUE_PASSES = 24
SELECT_MAX_PASSES = 64
VMEM_LIMIT = 48 * 1024 * 1024
FFN_VMEM_LIMIT = 58 * 1024 * 1024

SEG_A = 2 * A_WIDTH
SEG_QB = B_HEADS * LANE
SEG_RET = 4 * C_WIDTH
SEG_QI = IDX_HEADS * LANE
SEG_KV = LANE
SEG_KW = LANE
NEG_INF_KEY = np.int32(np.uint32(0x807FFFFF).astype(np.int64) - (1 << 32))
INT_MIN = np.int32(-(2 ** 31))
MIN_NORMAL_KEY = np.int32(0x00800000)


def _dot(a, b):
    return jnp.dot(a, b, preferred_element_type=F32)


def _dot_nt(a, b):
    return lax.dot_general(a, b, (((1,), (1,)), ((), ())), preferred_element_type=F32)


def _dot_tn(a, b):
    return lax.dot_general(a, b, (((0,), (0,)), ((), ())), preferred_element_type=F32)


def _split_dot(x, m):
    hi = x.astype(BF16)
    lo = (x - hi.astype(F32)).astype(BF16)
    return _dot(hi, m) + _dot(lo, m)


def _group_norm(x, gmat):
    mu = _split_dot(x, gmat)
    d = x - mu
    var = _split_dot(d * d, gmat)
    return d * lax.rsqrt(var + LN_EPS)


def _layer_norm(x, g, b):
    mu = jnp.mean(x, axis=-1, keepdims=True)
    d = x - mu
    var = jnp.mean(d * d, axis=-1, keepdims=True)
    return d * lax.rsqrt(var + LN_EPS) * g + b


def _params(n_axes, vmem_limit=VMEM_LIMIT):
    return pltpu.CompilerParams(dimension_semantics=("arbitrary",) * n_axes,
                                vmem_limit_bytes=vmem_limit)


def _proj_kernel(x_ref, w_ref, a_ref, qb_ref, ret_ref, qi_ref, kv_ref, kw_ref, kvb_ref, kwb_ref):
    xb = x_ref[...].astype(BF16)
    o = 0
    a_ref[...] = _dot(xb, w_ref[:, o:o + SEG_A]); o += SEG_A
    qb_ref[...] = _dot(xb, w_ref[:, o:o + SEG_QB]).astype(BF16); o += SEG_QB
    ret_ref[...] = _dot(xb, w_ref[:, o:o + SEG_RET]); o += SEG_RET
    qi_ref[...] = _dot(xb, w_ref[:, o:o + SEG_QI]).astype(BF16); o += SEG_QI
    kv = _dot(xb, w_ref[:, o:o + SEG_KV]); o += SEG_KV
    kw = _dot(xb, w_ref[:, o:o + SEG_KW])
    kv_ref[...] = kv
    kw_ref[...] = kw
    kvb_ref[...] = kv.astype(BF16)
    kwb_ref[...] = kw.astype(BF16)


def _proj(x2d, w_proj, tm):
    t = x2d.shape[0]
    n_cols = w_proj.shape[1]
    widths = (SEG_A, SEG_QB, SEG_RET, SEG_QI, SEG_KV, SEG_KW, SEG_KV, SEG_KW)
    dtypes = (F32, BF16, F32, BF16, F32, F32, BF16, BF16)
    return pl.pallas_call(
        _proj_kernel,
        grid=(t // tm,),
        in_specs=[pl.BlockSpec((tm, D_MODEL), lambda i: (i, 0)),
                  pl.BlockSpec((D_MODEL, n_cols), lambda i: (0, 0))],
        out_specs=[pl.BlockSpec((tm, w), lambda i: (i, 0)) for w in widths],
        out_shape=[jax.ShapeDtypeStruct((t, w), d) for w, d in zip(widths, dtypes)],
        compiler_params=_params(1),
        name="proj",
    )(x2d, w_proj)


def _mixa_kernel(pa_ref, wcat_ref, bias_ref, g_ref, b_ref, gmat_ref, oa_ref, av_ref, *, chunk, rows):
    pa = pa_ref[...]
    u = jax.nn.gelu(pa[:, :A_WIDTH])
    v = jax.nn.gelu(pa[:, A_WIDTH:])
    vn = _group_norm(v, gmat_ref[...]) * g_ref[...] + b_ref[...]
    av_ref[...] = vn
    group = lax.broadcasted_iota(I32, (chunk, A_WIDTH), 1) // HEAD_DIM
    wcat = wcat_ref[...]
    bias = bias_ref[...]
    for c in range(rows // chunk):
        vc = vn[c * chunk:(c + 1) * chunk, :]
        stacked = jnp.concatenate(
            [jnp.where(group == g, vc, 0.0).astype(BF16) for g in range(A_GROUPS)], axis=0)
        s = _dot(wcat, stacked) + bias
        oa_ref[c * chunk:(c + 1) * chunk, :] = u[c * chunk:(c + 1) * chunk, :] * s


def _mixa(pa, wcat, bias, ln_g, ln_b, gmat, chunk, rows):
    t = pa.shape[0]
    const = lambda i: (0, 0)
    return pl.pallas_call(
        functools.partial(_mixa_kernel, chunk=chunk, rows=rows),
        grid=(t // rows,),
        in_specs=[pl.BlockSpec((rows, SEG_A), lambda i: (i, 0)),
                  pl.BlockSpec(wcat.shape, const),
                  pl.BlockSpec(bias.shape, const),
                  pl.BlockSpec((1, A_WIDTH), const),
                  pl.BlockSpec((1, A_WIDTH), const),
                  pl.BlockSpec(gmat.shape, const)],
        out_specs=[pl.BlockSpec((rows, A_WIDTH), lambda i: (i, 0)),
                   pl.BlockSpec((rows, A_WIDTH), lambda i: (i, 0))],
        out_shape=[jax.ShapeDtypeStruct((t, A_WIDTH), F32),
                   jax.ShapeDtypeStruct((t, A_WIDTH), F32)],
        compiler_params=_params(1),
        name="mixa",
    )(pa, wcat, bias, ln_g, ln_b, gmat)


def _swap_halves(x):
    n = x.shape[1]
    lane = lax.broadcasted_iota(I32, x.shape, 1)
    fwd = pltpu.roll(x, n - HEAD_DIM // 2, axis=1)
    bwd = pltpu.roll(x, HEAD_DIM // 2, axis=1)
    return jnp.where(lane % HEAD_DIM < HEAD_DIM // 2, fwd, bwd)


def _ret_kernel(pr_ref, cos_ref, sin_ref, dec_ref, qd_ref, kd_ref, cd_ref, bd_ref, gng_ref, gmat_ref,
                r0_ref, oc_ref, rout_ref, state_ref, *, chunk, rows):
    @pl.when(pl.program_id(1) == 0)
    def _():
        state_ref[...] = r0_ref[0]

    lane = lax.broadcasted_iota(I32, (chunk, LANE), 1)
    first = (lane < HEAD_DIM).astype(F32)
    second = 1.0 - first
    qd = qd_ref[...]
    kd = kd_ref[...]
    ys = []
    for c in range(rows // chunk):
        sl = slice(c * chunk, (c + 1) * chunk)
        blk = pr_ref[0, sl, :]
        cos = jnp.concatenate([cos_ref[sl, :]] * 3, axis=1)
        sin = jnp.concatenate([sin_ref[sl, :]] * 3, axis=1)
        q = blk[:, 0:C_WIDTH]
        k = blk[:, C_WIDTH:2 * C_WIDTH]
        v = blk[:, 2 * C_WIDTH:3 * C_WIDTH]
        q = q * cos + _swap_halves(q) * sin
        k = (k * cos + _swap_halves(k) * sin) * (HEAD_DIM ** -0.5)
        kdec = k * kd
        pairs = []
        for p in range(C_HEADS // 2):
            ls = slice(p * LANE, (p + 1) * LANE)
            qp, kp, vp = q[:, ls], k[:, ls], v[:, ls]
            kpb = kp.astype(BF16)
            intra = jnp.zeros((chunk, LANE), F32)
            for half, m in ((0, first), (1, second)):
                s = _dot_nt((qp * m).astype(BF16), kpb) * dec_ref[2 * p + half]
                intra = intra + _dot(s.astype(BF16), (vp * m).astype(BF16))
            st = state_ref[p]
            cross = _dot(qp.astype(BF16), st.astype(BF16)) * qd[:, ls]
            kv = _dot_tn(kdec[:, ls].astype(BF16), vp.astype(BF16))
            state_ref[p] = cd_ref[p] * st + kv * bd_ref[...]
            pairs.append(intra + cross)
        ys.append(jnp.concatenate(pairs, axis=1))
    y = jnp.concatenate(ys, axis=0)
    yn = _group_norm(y, gmat_ref[...]) * gng_ref[...]
    g = pr_ref[0, :, 3 * C_WIDTH:4 * C_WIDTH]
    oc_ref[0] = jax.nn.silu(g) * yn
    rout_ref[0] = state_ref[...]


def _ret(pr, cos, sin, dec, qd, kd, cd, bd, gn_g, gmat, r0, chunk, rows):
    nb, t, _ = pr.shape
    npair = C_HEADS // 2
    c2 = lambda b, i: (0, 0)
    c3 = lambda b, i: (0, 0, 0)
    return pl.pallas_call(
        functools.partial(_ret_kernel, chunk=chunk, rows=rows),
        grid=(nb, t // rows),
        in_specs=[pl.BlockSpec((1, rows, SEG_RET), lambda b, i: (b, i, 0)),
                  pl.BlockSpec((rows, LANE), lambda b, i: (i, 0)),
                  pl.BlockSpec((rows, LANE), lambda b, i: (i, 0)),
                  pl.BlockSpec(dec.shape, c3),
                  pl.BlockSpec(qd.shape, c2),
                  pl.BlockSpec(kd.shape, c2),
                  pl.BlockSpec(cd.shape, c3),
                  pl.BlockSpec(bd.shape, c2),
                  pl.BlockSpec((1, C_WIDTH), c2),
                  pl.BlockSpec(gmat.shape, c2),
                  pl.BlockSpec((1, npair, LANE, LANE), lambda b, i: (b, 0, 0, 0))],
        out_specs=[pl.BlockSpec((1, rows, C_WIDTH), lambda b, i: (b, i, 0)),
                   pl.BlockSpec((1, npair, LANE, LANE), lambda b, i: (b, 0, 0, 0))],
        out_shape=[jax.ShapeDtypeStruct((nb, t, C_WIDTH), F32),
                   jax.ShapeDtypeStruct((nb, npair, LANE, LANE), F32)],
        scratch_shapes=[pltpu.VMEM((npair, LANE, LANE), F32)],
        compiler_params=_params(2),
        name="ret",
    )(pr, cos, sin, dec, qd, kd, cd, bd, gn_g, gmat, r0)


def _lane_blocks(x):
    return [x[:, i * LANE:(i + 1) * LANE] for i in range(x.shape[1] // LANE)]


def _key_to_float(key):
    return pltpu.bitcast(key ^ ((key >> 31) & 0x7FFFFFFF), F32)


def _threshold_float(key):
    subnormal = jnp.logical_and(key > 0, key < MIN_NORMAL_KEY)
    return _key_to_float(jnp.where(subnormal, MIN_NORMAL_KEY, key))


def _float_to_key(x):
    bits = pltpu.bitcast(x, I32)
    return bits ^ ((bits >> 31) & 0x7FFFFFFF)


def _dsa_kernel(qb_ref, qi_ref, kw_ref, kv_ref, kx_ref, tri_ref, o_ref,
                sc_ref, wb_ref, m_ref, l_ref, acc_ref, mx_ref, mn_ref,
                *, groups, tg, nsb_total, causal, k_sel, n_keys):
    ks = KEY_SUPER
    nrep = ks // LANE
    tq = groups * tg
    if causal:
        j = pl.program_id(1)
        nsb = (j * tq) // ks + 1
        row = lax.broadcasted_iota(I32, (tq, 1), 0)
        limit = ((j * tq + row) // CHUNK + 1) * CHUNK
    else:
        nsb = nsb_total
        limit = jnp.full((tq, 1), n_keys, I32)
    lane = lax.broadcasted_iota(I32, (tq, ks), 1)

    def wide(x):
        return jnp.concatenate([x] * nrep, axis=1)

    def stack_heads(ref, g, n_heads):
        return jnp.concatenate([ref[g][:, h * LANE:(h + 1) * LANE] for h in range(n_heads)], axis=0)

    def group_rows(per_group):
        return jnp.concatenate(per_group, axis=0) if groups > 1 else per_group[0]

    qs = [stack_heads(qb_ref, g, B_HEADS) for g in range(groups)]
    qis = [stack_heads(qi_ref, g, IDX_HEADS) for g in range(groups)]

    for g in range(groups):
        w = kw_ref[g][:, IDX_DIM:IDX_DIM + IDX_HEADS] * (IDX_DIM ** -0.5 * IDX_HEADS ** -0.5)
        for h in range(IDX_HEADS):
            wb_ref[h, g * tg:(g + 1) * tg, :] = jnp.broadcast_to(w[:, h:h + 1], (tg, LANE))

    mx_ref[...] = jnp.full(mx_ref.shape, -jnp.inf, F32)
    mn_ref[...] = jnp.full(mn_ref.shape, jnp.inf, F32)

    def score_body(sb, carry):
        off = pl.multiple_of(sb * ks, ks)
        accs = []
        for g in range(groups):
            d = _dot_nt(qis[g], kx_ref[g, pl.ds(off, ks), :])
            acc = jnp.zeros((tg, ks), F32)
            for h in range(IDX_HEADS):
                acc = acc + jnp.maximum(d[h * tg:(h + 1) * tg], 0.0) * wide(wb_ref[h, g * tg:(g + 1) * tg, :])
            accs.append(acc)
        acc = group_rows(accs)
        adm = off + lane < limit
        sc = jnp.where(adm, acc, -jnp.inf)
        sc_ref[sb] = sc
        mx_ref[...] = functools.reduce(jnp.maximum, _lane_blocks(sc), mx_ref[...])
        mn_ref[...] = functools.reduce(jnp.minimum, _lane_blocks(jnp.where(adm, acc, jnp.inf)), mn_ref[...])
        return carry

    lax.fori_loop(0, nsb, score_body, 0)

    rows = min(tq, LANE)
    dense = tq % LANE == 0

    def to_state(per_row_lanes, reduce_fn):
        if dense:
            return reduce_fn(per_row_lanes.T, axis=0, keepdims=True)
        return reduce_fn(per_row_lanes, axis=1, keepdims=True)

    def to_rows(state, r0):
        if dense:
            return jnp.broadcast_to(state[:, r0:r0 + rows], (rows, LANE)).T
        return jnp.broadcast_to(state, (rows, LANE))

    def gather_state(parts):
        return jnp.concatenate(parts, axis=1 if dense else 0) if len(parts) > 1 else parts[0]

    def count(cand, strict):
        starts = range(0, tq, rows)
        cbs = [to_rows(cand, r0) for r0 in starts]
        accs = []
        for r0, cb in zip(starts, cbs):
            def body(sb, acc, r0=r0, cb=cb):
                for i in range(nrep):
                    x = sc_ref[sb, r0:r0 + rows, i * LANE:(i + 1) * LANE]
                    acc = acc + jnp.where(x > cb if strict else x >= cb, 1.0, 0.0)
                return acc

            accs.append(lax.fori_loop(0, nsb, body, jnp.zeros((rows, LANE), F32)))
        return gather_state([to_state(acc, jnp.sum) for acc in accs])

    if causal and dense:
        pos = j * tq + lax.broadcasted_iota(I32, (1, tq), 1)
        n_adm = ((pos // CHUNK + 1) * CHUNK).astype(F32)
    else:
        n_adm = jnp.minimum(limit, n_keys).astype(F32)
    row_min = gather_state([to_state(mn_ref[r0:r0 + rows], jnp.min) for r0 in range(0, tq, rows)])
    row_max = gather_state([to_state(mx_ref[r0:r0 + rows], jnp.max) for r0 in range(0, tq, rows)])
    short = n_adm <= k_sel
    lo0 = jnp.where(short, NEG_INF_KEY + 1, _float_to_key(row_min))
    hi0 = jnp.where(short, NEG_INF_KEY + 2, _float_to_key(row_max) + 1)
    c0 = jnp.where(short, float(k_sel), n_adm)

    def unresolved(lo, hi, c_lo):
        return jnp.logical_and(hi > lo + 1, c_lo != k_sel)

    def any_unresolved(lo, hi, c_lo):
        return jnp.max(jnp.where(unresolved(lo, hi, c_lo), 1.0, 0.0))

    def value_mid(lo, hi):
        return _float_to_key(0.5 * _key_to_float(lo) + 0.5 * _key_to_float(hi))

    def step(lo, hi, c_lo, cand, strict):
        live = unresolved(lo, hi, c_lo)
        cand = jnp.minimum(jnp.maximum(cand, lo + 1 - strict), hi - 1 - strict)
        cnt = count(_threshold_float(cand), bool(strict))
        edge = cand + strict
        take = jnp.logical_and(live, cnt >= k_sel)
        drop = jnp.logical_and(live, cnt < k_sel)
        return jnp.where(take, edge, lo), jnp.where(drop, edge, hi), jnp.where(take, cnt, c_lo)

    lo, hi, c_lo = step(lo0, hi0, c0, jnp.where(jnp.logical_and(lo0 < 0, hi0 > 0), 0, value_mid(lo0, hi0)), 0)
    lo, hi, c_lo = step(lo, hi, c_lo, jnp.where(lo == 0, 0, value_mid(lo, hi)), 1)

    def sel_cond(carry):
        return jnp.logical_and(carry[0] < SELECT_MAX_PASSES, carry[1] > 0.5)

    def sel_body(carry):
        it, _, lo, hi, c_lo = carry
        active = any_unresolved(lo, hi, c_lo)
        kmid = (lo >> 1) + (hi >> 1) + (lo & hi & 1)
        lo, hi, c_lo = step(lo, hi, c_lo, jnp.where(it < SELECT_VALUE_PASSES, value_mid(lo, hi), kmid), 0)
        return it + 1, active, lo, hi, c_lo

    _, _, lo, _, c_lo = lax.while_loop(
        sel_cond, sel_body, (jnp.int32(0), any_unresolved(lo, hi, c_lo), lo, hi, c_lo))
    thr = _threshold_float(lo)

    def rows_of(state):
        return jnp.concatenate([to_rows(state, r0) for r0 in range(0, tq, rows)], axis=0)

    thr_rows = wide(rows_of(thr))

    any_tied = jnp.max(jnp.where(c_lo > k_sel, 1.0, 0.0)) > 0.5

    @pl.when(any_tied)
    def _():
        need = rows_of(k_sel - count(thr, True))[:, :1]

        def body(sb, before):
            sc = sc_ref[sb]
            eq = sc == thr_rows
            eqf = jnp.where(eq, 1.0, 0.0)
            prefix = _dot(eqf.astype(BF16), tri_ref[...]) + before
            drop = jnp.logical_and(eq, prefix >= need)
            sc_ref[sb] = jnp.where(drop, -jnp.inf, sc)
            return before + jnp.sum(eqf, axis=1, keepdims=True)

        lax.fori_loop(0, nsb, body, jnp.zeros((tq, 1), F32))

    def bias_body(sb, carry):
        sc_ref[sb] = jnp.where(sc_ref[sb] >= thr_rows, 0.0, -jnp.inf)
        return carry

    lax.fori_loop(0, nsb, bias_body, 0)

    def logits(sb):
        off = pl.multiple_of(sb * ks, ks)
        kvs = [kv_ref[g, pl.ds(off, ks), :] for g in range(groups)]
        s = [_dot_nt(qs[g], kvs[g]) for g in range(groups)]
        return [group_rows([s[g][h * tg:(h + 1) * tg] for g in range(groups)]) for h in range(B_HEADS)], kvs

    m_ref[...] = jnp.full(m_ref.shape, -jnp.inf, F32)

    def max_body(sb, carry):
        s, _ = logits(sb)
        bias = sc_ref[sb]
        for h in range(B_HEADS):
            m_ref[h] = functools.reduce(jnp.maximum, _lane_blocks(s[h] + bias), m_ref[h])
        return carry

    lax.fori_loop(0, nsb, max_body, 0)
    for h in range(B_HEADS):
        m_ref[h] = jnp.broadcast_to(jnp.max(m_ref[h], axis=1, keepdims=True), (tq, LANE))
    l_ref[...] = jnp.zeros(l_ref.shape, F32)
    acc_ref[...] = jnp.zeros(acc_ref.shape, F32)

    def pv_body(sb, carry):
        s, kvs = logits(sb)
        bias = sc_ref[sb]
        ps = []
        for h in range(B_HEADS):
            p = jnp.exp(s[h] + bias - wide(m_ref[h]))
            l_ref[h] += sum(_lane_blocks(p))
            ps.append(p.astype(BF16))
        for g in range(groups):
            pg = jnp.concatenate([ps[h][g * tg:(g + 1) * tg] for h in range(B_HEADS)], axis=0)
            acc_ref[g * B_HEADS * tg:(g + 1) * B_HEADS * tg, :] += _dot(pg, kvs[g])
        return carry

    lax.fori_loop(0, nsb, pv_body, 0)

    lane_head = lax.broadcasted_iota(I32, (tg, LANE), 1)
    for g in range(groups):
        outs = [acc_ref[(g * B_HEADS + h) * tg:(g * B_HEADS + h + 1) * tg, :]
                / jnp.sum(l_ref[h, g * tg:(g + 1) * tg, :], axis=1, keepdims=True) for h in range(B_HEADS)]
        for p in range(B_HEADS // 2):
            even = pltpu.roll(outs[2 * p], HEAD_DIM, axis=1)
            o_ref[g, :, p * LANE:(p + 1) * LANE] = jnp.where(lane_head < HEAD_DIM, even, outs[2 * p + 1])


def _dsa(qb, qi, kw, kvb, kxb, tri, groups, tg, causal, n_keys):
    nb, t, _ = qb.shape
    nk = kvb.shape[1]
    nsb_total = nk // KEY_SUPER
    tq = groups * tg
    assert nb % groups == 0 and t % tg == 0 and not (causal and groups > 1)
    kernel = functools.partial(_dsa_kernel, groups=groups, tg=tg, nsb_total=nsb_total, causal=causal,
                               k_sel=min(TOPK_MAX, n_keys // 4), n_keys=n_keys)
    return pl.pallas_call(
        kernel,
        grid=(nb // groups, t // tg),
        in_specs=[pl.BlockSpec((groups, tg, SEG_QB), lambda b, j: (b, j, 0)),
                  pl.BlockSpec((groups, tg, SEG_QI), lambda b, j: (b, j, 0)),
                  pl.BlockSpec((groups, tg, SEG_KW), lambda b, j: (b, j, 0)),
                  pl.BlockSpec((groups, nk, SEG_KV), lambda b, j: (b, 0, 0)),
                  pl.BlockSpec((groups, nk, SEG_KW), lambda b, j: (b, 0, 0)),
                  pl.BlockSpec(tri.shape, lambda b, j: (0, 0))],
        out_specs=pl.BlockSpec((groups, tg, B_WIDTH), lambda b, j: (b, j, 0)),
        out_shape=jax.ShapeDtypeStruct((nb, t, B_WIDTH), F32),
        scratch_shapes=[pltpu.VMEM((nsb_total, tq, KEY_SUPER), F32),
                        pltpu.VMEM((IDX_HEADS, tq, LANE), F32),
                        pltpu.VMEM((B_HEADS, tq, LANE), F32),
                        pltpu.VMEM((B_HEADS, tq, LANE), F32),
                        pltpu.VMEM((B_HEADS * tq, LANE), F32),
                        pltpu.VMEM((tq, LANE), F32),
                        pltpu.VMEM((tq, LANE), F32)],
        compiler_params=_params(2),
        name="dsa",
    )(qb, qi, kw, kvb, kxb, tri)


def _ffn_kernel(*refs, tm, alpha, streaming, seq):
    if streaming:
        (x_ref, oa_ref, ob_ref, oc_ref, wo_ref, g1_ref, b1_ref, wg_ref, wu_ref, cw_ref, cb_ref, wd_ref,
         g2_ref, b2_ref, y_ref, cs_ref, x1_ref, x1b_ref, acc_ref, carry_ref) = refs
    else:
        (x_ref, oa_ref, ob_ref, oc_ref, wo_ref, g1_ref, b1_ref, wg_ref, wu_ref, cw_ref, cb_ref, wd_ref,
         g2_ref, b2_ref, p1_ref, p2_ref, y_ref, cs_ref, x1_ref, x1b_ref, acc_ref) = refs
    f = pl.program_id(1)
    nf = pl.num_programs(1)

    @pl.when(f == 0)
    def _():
        mix = (_dot(oa_ref[...].astype(BF16), wo_ref[0:A_WIDTH, :])
               + _dot(ob_ref[...].astype(BF16), wo_ref[A_WIDTH:A_WIDTH + B_WIDTH, :])
               + _dot(oc_ref[...].astype(BF16), wo_ref[A_WIDTH + B_WIDTH:, :]))
        x1 = _layer_norm(alpha * x_ref[...] + mix, g1_ref[...], b1_ref[...])
        x1_ref[...] = x1
        x1b_ref[...] = x1.astype(BF16)
        acc_ref[...] = jnp.zeros_like(acc_ref)

    xb = x1b_ref[...]
    hg = _dot(xb, wg_ref[...])
    hu = _dot(xb, wu_ref[...])
    tf = hg.shape[1]
    h1 = pltpu.roll(hg, 1, axis=0)
    h2 = pltpu.roll(hg, 2, axis=0)
    if streaming:
        @pl.when((pl.program_id(0) * tm) % seq == 0)
        def _():
            carry_ref[f] = jnp.zeros((8, tf), F32)

        prev = carry_ref[f]
        c0 = prev[6:7, :]
        c1 = prev[7:8, :]
        top = lax.broadcasted_iota(I32, (8, tf), 0)
        h1 = jnp.concatenate([jnp.where(top == 0, c1, h1[:8]), h1[8:]], axis=0)
        h2 = jnp.concatenate([jnp.where(top == 0, c0, jnp.where(top == 1, c1, h2[:8])), h2[8:]], axis=0)
        carry_ref[f] = hg[tm - 8:, :]
        cs_ref[0] = hg[tm - 8:, :]
    else:
        pos = lax.broadcasted_iota(I32, (tm, tf), 0) % seq
        h1 = jnp.where(pos == 0, p1_ref[...], h1)
        h2 = jnp.where(pos < 2, p2_ref[...], h2)
        cs_ref[...] = hg
    cw = cw_ref[...]
    conv = cb_ref[...] + cw[0:1, :] * h2 + cw[1:2, :] * h1 + cw[2:3, :] * hg
    act = jax.nn.gelu(conv) * hu
    acc_ref[...] += _dot(act.astype(BF16), wd_ref[...])

    @pl.when(f == nf - 1)
    def _():
        y_ref[...] = _layer_norm(alpha * x1_ref[...] + acc_ref[...], g2_ref[...], b2_ref[...])


def _ffn(x2d, oa, ob, oc, lw, tm, tf, seq, prev=None):
    t = x2d.shape[0]
    nf = D_FF // tf
    streaming = prev is None
    alpha = lw["alpha"]
    row = lambda i, f: (i, 0)
    const = lambda i, f: (0, 0)
    in_specs = [pl.BlockSpec((tm, D_MODEL), row),
                pl.BlockSpec((tm, A_WIDTH), row),
                pl.BlockSpec((tm, B_WIDTH), row),
                pl.BlockSpec((tm, C_WIDTH), row),
                pl.BlockSpec((D_MODEL, D_MODEL), const),
                pl.BlockSpec((1, D_MODEL), const),
                pl.BlockSpec((1, D_MODEL), const),
                pl.BlockSpec((D_MODEL, tf), lambda i, f: (0, f)),
                pl.BlockSpec((D_MODEL, tf), lambda i, f: (0, f)),
                pl.BlockSpec((CONV_W, tf), lambda i, f: (0, f)),
                pl.BlockSpec((1, tf), lambda i, f: (0, f)),
                pl.BlockSpec((tf, D_MODEL), lambda i, f: (f, 0)),
                pl.BlockSpec((1, D_MODEL), const),
                pl.BlockSpec((1, D_MODEL), const)]
    args = [x2d, oa, ob, oc, lw["w_out"], lw["ln1_g"], lw["ln1_b"], lw["w_gate"], lw["w_up"],
            lw["conv_w"], lw["conv_b"], lw["w_down"], lw["ln2_g"], lw["ln2_b"]]
    scratch = [pltpu.VMEM((tm, D_MODEL), F32), pltpu.VMEM((tm, D_MODEL), BF16),
               pltpu.VMEM((tm, D_MODEL), F32)]
    if streaming:
        cs_spec = pl.BlockSpec((1, 8, tf), lambda i, f: (i, 0, f))
        cs_shape = jax.ShapeDtypeStruct((t // tm, 8, D_FF), F32)
        scratch.append(pltpu.VMEM((nf, 8, tf), F32))
    else:
        in_specs += [pl.BlockSpec((tm, tf), lambda i, f: (i, f))] * 2
        args += list(prev)
        cs_spec = pl.BlockSpec((tm, tf), lambda i, f: (i, f))
        cs_shape = jax.ShapeDtypeStruct((t, D_FF), F32)
    return pl.pallas_call(
        functools.partial(_ffn_kernel, tm=tm, alpha=alpha, streaming=streaming, seq=seq),
        grid=(t // tm, nf),
        in_specs=in_specs,
        out_specs=[pl.BlockSpec((tm, D_MODEL), row), cs_spec],
        out_shape=[jax.ShapeDtypeStruct((t, D_MODEL), F32), cs_shape],
        scratch_shapes=scratch,
        compiler_params=_params(2, FFN_VMEM_LIMIT),
        name="ffn",
    )(*args)


def _pad_cols(w, width):
    return jnp.pad(w, ((0, 0), (0, width - w.shape[1])))


def _proj_weight(w_in):
    sizes = (A_WIDTH, A_WIDTH, B_WIDTH, HEAD_DIM, HEAD_DIM, IDX_HEADS * IDX_DIM, IDX_DIM, IDX_HEADS,
             C_WIDTH, C_WIDTH, C_WIDTH, C_WIDTH)
    splits = np.cumsum(sizes)[:-1].tolist()
    ua, va, qb, kb, vb, qib, kib, wib, qc, kc, vc, gc = jnp.split(w_in, splits, axis=1)
    qb = qb * (HEAD_DIM ** -0.5)
    cols = [ua, va]
    cols += [_pad_cols(qb[:, h * HEAD_DIM:(h + 1) * HEAD_DIM], LANE) for h in range(B_HEADS)]
    cols += [qc, kc, vc, gc]
    cols += [_pad_cols(qib[:, h * IDX_DIM:(h + 1) * IDX_DIM], LANE) for h in range(IDX_HEADS)]
    cols += [kb, vb, _pad_cols(jnp.concatenate([kib, wib], axis=1), LANE)]
    return jnp.concatenate(cols, axis=1).astype(BF16)


def _group_matrix(width):
    g = np.arange(width) // HEAD_DIM
    return jnp.asarray((g[:, None] == g[None, :]).astype(np.float32) / HEAD_DIM, BF16)


def _rotary_tables(pos):
    half = HEAD_DIM // 2
    freqs = ROPE_BASE ** (-jnp.arange(half, dtype=F32) / half)
    ang = pos.astype(F32)[:, None] * freqs
    cos, sin = jnp.cos(ang), jnp.sin(ang)
    cos = jnp.concatenate([cos, cos, cos, cos], axis=1)
    sin = jnp.concatenate([-sin, sin, -sin, sin], axis=1)
    return cos, sin


def _retention_tables(c):
    log_g = jnp.log(1.0 - 2.0 ** (-5.0 - jnp.arange(C_HEADS, dtype=F32)))
    i = jnp.arange(c, dtype=F32)
    diff = i[:, None] - i[None, :]
    dec = jnp.where(diff >= 0, jnp.exp(jnp.maximum(diff, 0.0)[None] * log_g[:, None, None]), 0.0)
    qd = jnp.repeat(jnp.exp((i + 1)[None] * log_g[:, None]).T, HEAD_DIM, axis=1)
    kd = jnp.repeat(jnp.exp((c - 1 - i)[None] * log_g[:, None]).T, HEAD_DIM, axis=1)
    cdl = jnp.repeat(jnp.exp(c * log_g), HEAD_DIM).reshape(C_HEADS // 2, LANE)
    cd = jnp.broadcast_to(cdl[:, :, None], (C_HEADS // 2, LANE, LANE))
    blk = np.arange(LANE) // HEAD_DIM
    bd = jnp.asarray((blk[:, None] == blk[None, :]).astype(np.float32))
    return dec, qd, kd, cd, bd


def _pair_states(r):
    n = r.shape[0]
    r = r.reshape(n, C_HEADS // 2, 2, HEAD_DIM, HEAD_DIM).astype(F32)
    z = jnp.zeros_like(r[:, :, 0])
    top = jnp.concatenate([r[:, :, 0], z], axis=-1)
    bot = jnp.concatenate([z, r[:, :, 1]], axis=-1)
    return jnp.concatenate([top, bot], axis=-2)


def _unpair_states(s):
    a = s[:, :, :HEAD_DIM, :HEAD_DIM]
    b = s[:, :, HEAD_DIM:, HEAD_DIM:]
    n = s.shape[0]
    return jnp.stack([a, b], axis=2).reshape(n, C_HEADS, HEAD_DIM, HEAD_DIM)


def _mixa_tables(ws, bs, c):
    tril = jnp.tril(jnp.ones((c, c), ws.dtype))
    w = ws[:, :c, :c] * tril
    wcat = jnp.concatenate([w[g] for g in range(A_GROUPS)], axis=1).astype(BF16)
    bias = jnp.repeat(bs[:, :c].T, HEAD_DIM, axis=1)
    return wcat, bias


def _layer_weights(l, depth, w_out, ln1_g, ln1_b, w_gate, w_up, conv_w, conv_b, w_down, ln2_g, ln2_b):
    return dict(alpha=float((2 * depth) ** 0.25),
                w_out=w_out[l].astype(BF16), ln1_g=ln1_g[l][None], ln1_b=ln1_b[l][None],
                w_gate=w_gate[l].astype(BF16), w_up=w_up[l].astype(BF16), conv_w=conv_w[l],
                conv_b=conv_b[l][None], w_down=w_down[l].astype(BF16),
                ln2_g=ln2_g[l][None], ln2_b=ln2_b[l][None])


def kernel(x_prompt, x_sample, cache_b_k, cache_b_v, cache_b_kidx, state_ret, state_ffn_conv,
           w_in, a_ln_g, a_ln_b, a_ws, a_bs, c_gn_g, w_out, ln1_g, ln1_b,
           w_gate, w_up, conv_w, conv_b, w_down, ln2_g, ln2_b):
    depth = w_in.shape[0]
    bp, s, _ = x_prompt.shape
    bs_, t, _ = x_sample.shape
    past = cache_b_k.shape[2]
    l_keys = past + t
    ts = bs_ * t
    tm_p = 512
    tm_s = ts if ts <= 512 else 512
    tf_p, tf_s = D_FF // 2, 256
    assert s % tm_p == 0 and s % A_CHUNK == 0 and ts % tm_s == 0 and tm_s % t == 0 and t >= 2
    assert s % KEY_SUPER == 0 and KEY_SUPER % QUERY_BLOCK == 0
    nk_s = -(-l_keys // KEY_SUPER) * KEY_SUPER
    groups_s = max(g for g in range(1, max(LANE // t, 1) + 1) if bs_ % g == 0)

    gmat_a = _group_matrix(A_WIDTH)
    gmat_c = _group_matrix(C_WIDTH)
    tri = jnp.asarray(np.triu(np.ones((KEY_SUPER, KEY_SUPER), np.float32), 1), BF16)
    cos_p, sin_p = _rotary_tables(jnp.arange(s))
    cos_s, sin_s = _rotary_tables(past + jnp.arange(t))
    ret_p = _retention_tables(CHUNK)
    ret_s = _retention_tables(t)

    xp = x_prompt.reshape(bp * s, D_MODEL)
    xs = x_sample.reshape(ts, D_MODEL)
    outs = {k: [] for k in ("kp", "vp", "kip", "rp", "cp", "ks", "vs", "kis", "rs", "cs", "avs")}
    for l in range(depth):
        w_proj = _proj_weight(w_in[l])
        lw = _layer_weights(l, depth, w_out, ln1_g, ln1_b, w_gate, w_up, conv_w, conv_b, w_down, ln2_g, ln2_b)
        ln_g, ln_b, gn_g = a_ln_g[l][None], a_ln_b[l][None], c_gn_g[l][None]

        pa, qb, pr, qi, kv, kw, kvb, kwb = _proj(xp, w_proj, tm_p)
        wcat, bias = _mixa_tables(a_ws[l], a_bs[l], A_CHUNK)
        oa, _ = _mixa(pa, wcat, bias, ln_g, ln_b, gmat_a, A_CHUNK, 512)
        r0 = jnp.zeros((bp, C_HEADS // 2, LANE, LANE), F32)
        oc, r_p = _ret(pr.reshape(bp, s, SEG_RET), cos_p, sin_p, *ret_p, gn_g, gmat_c, r0, CHUNK, 512)
        ob = _dsa(qb.reshape(bp, s, SEG_QB), qi.reshape(bp, s, SEG_QI), kw.reshape(bp, s, SEG_KW),
                  kvb.reshape(bp, s, SEG_KV), kwb.reshape(bp, s, SEG_KW), tri, 1, QUERY_BLOCK, True, s)
        xp, c_p = _ffn(xp, oa, ob.reshape(bp * s, B_WIDTH), oc.reshape(bp * s, C_WIDTH), lw, tm_p, tf_p, s)
        kv3 = kv.reshape(bp, s, SEG_KV)
        outs["kp"].append(kv3[..., :HEAD_DIM])
        outs["vp"].append(kv3[..., HEAD_DIM:])
        outs["kip"].append(kw.reshape(bp, s, SEG_KW)[..., :IDX_DIM])
        outs["rp"].append(_unpair_states(r_p))
        outs["cp"].append(c_p[s // tm_p - 1::s // tm_p, 8 - (CONV_W - 1):, :])

        pa, qb, pr, qi, kv, kw, kvb, kwb = _proj(xs, w_proj, tm_s)
        wcat, bias = _mixa_tables(a_ws[l], a_bs[l], t)
        oa, av = _mixa(pa, wcat, bias, ln_g, ln_b, gmat_a, t, tm_s)
        r0 = _pair_states(state_ret[l])
        rows_s = t
        oc, r_s = _ret(pr.reshape(bs_, t, SEG_RET), cos_s, sin_s, *ret_s, gn_g, gmat_c, r0, t, rows_s)
        pad = nk_s - l_keys
        kv_full = jnp.concatenate(
            [jnp.concatenate([cache_b_k[l], cache_b_v[l]], axis=-1).astype(BF16),
             kvb.reshape(bs_, t, SEG_KV), jnp.zeros((bs_, pad, SEG_KV), BF16)], axis=1)
        kx_full = jnp.concatenate(
            [jnp.pad(cache_b_kidx[l], ((0, 0), (0, 0), (0, SEG_KW - IDX_DIM))).astype(BF16),
             kwb.reshape(bs_, t, SEG_KW), jnp.zeros((bs_, pad, SEG_KW), BF16)], axis=1)
        ob = _dsa(qb.reshape(bs_, t, SEG_QB), qi.reshape(bs_, t, SEG_QI), kw.reshape(bs_, t, SEG_KW),
                  kv_full, kx_full, tri, groups_s, t, False, l_keys)
        cprev = state_ffn_conv[l]
        zeros = jnp.zeros((bs_, t - 2, D_FF), F32)
        p1 = jnp.concatenate([cprev[:, 1:2], zeros, zeros[:, :1]], axis=1).reshape(ts, D_FF)
        p2 = jnp.concatenate([cprev, zeros], axis=1).reshape(ts, D_FF)
        xs, hg_s = _ffn(xs, oa, ob.reshape(ts, B_WIDTH), oc.reshape(ts, C_WIDTH), lw, tm_s, tf_s, t, (p1, p2))
        kv3 = kv.reshape(bs_, t, SEG_KV)
        outs["ks"].append(kv3[..., :HEAD_DIM])
        outs["vs"].append(kv3[..., HEAD_DIM:])
        outs["kis"].append(kw.reshape(bs_, t, SEG_KW)[..., :IDX_DIM])
        outs["rs"].append(_unpair_states(r_s))
        outs["cs"].append(hg_s.reshape(bs_, t, D_FF)[:, t - (CONV_W - 1):, :])
        outs["avs"].append(av.reshape(bs_, t, A_WIDTH))

    st = lambda k: jnp.stack(outs[k])
    return (xp.reshape(bp, s, D_MODEL), xs.reshape(bs_, t, D_MODEL),
            st("kp"), st("vp"), st("kip"), st("rp"), st("cp"),
            st("ks"), st("vs"), st("kis"), st("rs"), st("cs"), st("avs"))
```

```python
import functools

import numpy as np
import jax
import jax.numpy as jnp
from jax import lax
from jax.experimental import pallas as pl
from jax.experimental.pallas import tpu as pltpu

F32 = jnp.float32
BF16 = jnp.bfloat16
I32 = jnp.int32

D_MODEL = 1024
CHUNK = 64
HEAD_DIM = 64
A_GROUPS = 4
C_HEADS = 6
B_HEADS = 6
A_WIDTH = A_GROUPS * HEAD_DIM
B_WIDTH = B_HEADS * HEAD_DIM
C_WIDTH = C_HEADS * HEAD_DIM
A_CHUNK = 128
IDX_HEADS = 8
IDX_DIM = 32
TOPK_MAX = 256
ROPE_BASE = 10000.0
D_FF = 2816
CONV_W = 3
LN_EPS = 1e-5

LANE = 128
QUERY_BLOCK = 512
KEY_SUPER = 4 * LANE
SELECT_VALUE_PASSES = 24
SELECT_MAX_PASSES = 64
SAFE_LOGIT_BOUND = 40.0
NORM_BOUND_SLACK = 1.02
VMEM_LIMIT = 48 * 1024 * 1024
FFN_VMEM_LIMIT = 58 * 1024 * 1024

SEG_A = 2 * A_WIDTH
SEG_QB = B_HEADS * LANE
SEG_RET = 4 * C_WIDTH
SEG_QI = IDX_HEADS * LANE
SEG_KV = LANE
SEG_KW = LANE
NEG_INF_KEY = np.int32(np.uint32(0x807FFFFF).astype(np.int64) - (1 << 32))
INT_MIN = np.int32(-(2 ** 31))
MIN_NORMAL_KEY = np.int32(0x00800000)


def _dot(a, b):
    return jnp.dot(a, b, preferred_element_type=F32)


def _dot_nt(a, b):
    return lax.dot_general(a, b, (((1,), (1,)), ((), ())), preferred_element_type=F32)


def _dot_tn(a, b):
    return lax.dot_general(a, b, (((0,), (0,)), ((), ())), preferred_element_type=F32)


def _split_dot(x, m):
    hi = x.astype(BF16)
    lo = (x - hi.astype(F32)).astype(BF16)
    return _dot(hi, m) + _dot(lo, m)


def _group_norm(x, gmat):
    mu = _split_dot(x, gmat)
    d = x - mu
    var = _split_dot(d * d, gmat)
    return d * lax.rsqrt(var + LN_EPS)


def _layer_norm(x, g, b):
    mu = jnp.mean(x, axis=-1, keepdims=True)
    d = x - mu
    var = jnp.mean(d * d, axis=-1, keepdims=True)
    return d * lax.rsqrt(var + LN_EPS) * g + b


def _params(n_axes, vmem_limit=VMEM_LIMIT):
    return pltpu.CompilerParams(dimension_semantics=("arbitrary",) * n_axes,
                                vmem_limit_bytes=vmem_limit)


def _proj_kernel(x_ref, w_ref, a_ref, qb_ref, ret_ref, qi_ref, kv_ref, kw_ref, kvb_ref, kwb_ref):
    xb = x_ref[...].astype(BF16)
    o = 0
    a_ref[...] = _dot(xb, w_ref[:, o:o + SEG_A]); o += SEG_A
    qb_ref[...] = _dot(xb, w_ref[:, o:o + SEG_QB]).astype(BF16); o += SEG_QB
    ret_ref[...] = _dot(xb, w_ref[:, o:o + SEG_RET]); o += SEG_RET
    qi_ref[...] = _dot(xb, w_ref[:, o:o + SEG_QI]).astype(BF16); o += SEG_QI
    kv = _dot(xb, w_ref[:, o:o + SEG_KV]); o += SEG_KV
    kw = _dot(xb, w_ref[:, o:o + SEG_KW])
    kv_ref[...] = kv
    kw_ref[...] = kw
    kvb_ref[...] = kv.astype(BF16)
    kwb_ref[...] = kw.astype(BF16)


def _proj(x2d, w_proj, tm):
    t = x2d.shape[0]
    n_cols = w_proj.shape[1]
    widths = (SEG_A, SEG_QB, SEG_RET, SEG_QI, SEG_KV, SEG_KW, SEG_KV, SEG_KW)
    dtypes = (F32, BF16, F32, BF16, F32, F32, BF16, BF16)
    return pl.pallas_call(
        _proj_kernel,
        grid=(t // tm,),
        in_specs=[pl.BlockSpec((tm, D_MODEL), lambda i: (i, 0)),
                  pl.BlockSpec((D_MODEL, n_cols), lambda i: (0, 0))],
        out_specs=[pl.BlockSpec((tm, w), lambda i: (i, 0)) for w in widths],
        out_shape=[jax.ShapeDtypeStruct((t, w), d) for w, d in zip(widths, dtypes)],
        compiler_params=_params(1),
        name="proj",
    )(x2d, w_proj)


def _mixa_kernel(pa_ref, wcat_ref, bias_ref, g_ref, b_ref, gmat_ref, oa_ref, av_ref, *, chunk, rows):
    pa = pa_ref[...]
    u = jax.nn.gelu(pa[:, :A_WIDTH])
    v = jax.nn.gelu(pa[:, A_WIDTH:])
    vn = _group_norm(v, gmat_ref[...]) * g_ref[...] + b_ref[...]
    av_ref[...] = vn
    group = lax.broadcasted_iota(I32, (chunk, A_WIDTH), 1) // HEAD_DIM
    wcat = wcat_ref[...]
    bias = bias_ref[...]
    for c in range(rows // chunk):
        vc = vn[c * chunk:(c + 1) * chunk, :]
        stacked = jnp.concatenate(
            [jnp.where(group == g, vc, 0.0).astype(BF16) for g in range(A_GROUPS)], axis=0)
        s = _dot(wcat, stacked) + bias
        oa_ref[c * chunk:(c + 1) * chunk, :] = u[c * chunk:(c + 1) * chunk, :] * s


def _mixa(pa, wcat, bias, ln_g, ln_b, gmat, chunk, rows):
    t = pa.shape[0]
    const = lambda i: (0, 0)
    return pl.pallas_call(
        functools.partial(_mixa_kernel, chunk=chunk, rows=rows),
        grid=(t // rows,),
        in_specs=[pl.BlockSpec((rows, SEG_A), lambda i: (i, 0)),
                  pl.BlockSpec(wcat.shape, const),
                  pl.BlockSpec(bias.shape, const),
                  pl.BlockSpec((1, A_WIDTH), const),
                  pl.BlockSpec((1, A_WIDTH), const),
                  pl.BlockSpec(gmat.shape, const)],
        out_specs=[pl.BlockSpec((rows, A_WIDTH), lambda i: (i, 0)),
                   pl.BlockSpec((rows, A_WIDTH), lambda i: (i, 0))],
        out_shape=[jax.ShapeDtypeStruct((t, A_WIDTH), F32),
                   jax.ShapeDtypeStruct((t, A_WIDTH), F32)],
        compiler_params=_params(1),
        name="mixa",
    )(pa, wcat, bias, ln_g, ln_b, gmat)


def _swap_halves(x):
    n = x.shape[1]
    lane = lax.broadcasted_iota(I32, x.shape, 1)
    fwd = pltpu.roll(x, n - HEAD_DIM // 2, axis=1)
    bwd = pltpu.roll(x, HEAD_DIM // 2, axis=1)
    return jnp.where(lane % HEAD_DIM < HEAD_DIM // 2, fwd, bwd)


def _ret_kernel(pr_ref, cos_ref, sin_ref, dec_ref, qd_ref, kd_ref, cd_ref, bd_ref, gng_ref, gmat_ref,
                r0_ref, oc_ref, rout_ref, state_ref, *, chunk, rows):
    @pl.when(pl.program_id(1) == 0)
    def _():
        state_ref[...] = r0_ref[0]

    lane = lax.broadcasted_iota(I32, (chunk, LANE), 1)
    first = (lane < HEAD_DIM).astype(F32)
    second = 1.0 - first
    qd = qd_ref[...]
    kd = kd_ref[...]
    ys = []
    for c in range(rows // chunk):
        sl = slice(c * chunk, (c + 1) * chunk)
        blk = pr_ref[0, sl, :]
        cos = jnp.concatenate([cos_ref[sl, :]] * 3, axis=1)
        sin = jnp.concatenate([sin_ref[sl, :]] * 3, axis=1)
        q = blk[:, 0:C_WIDTH]
        k = blk[:, C_WIDTH:2 * C_WIDTH]
        v = blk[:, 2 * C_WIDTH:3 * C_WIDTH]
        q = q * cos + _swap_halves(q) * sin
        k = (k * cos + _swap_halves(k) * sin) * (HEAD_DIM ** -0.5)
        kdec = k * kd
        pairs = []
        for p in range(C_HEADS // 2):
            ls = slice(p * LANE, (p + 1) * LANE)
            qp, kp, vp = q[:, ls], k[:, ls], v[:, ls]
            kpb = kp.astype(BF16)
            intra = jnp.zeros((chunk, LANE), F32)
            for half, m in ((0, first), (1, second)):
                s = _dot_nt((qp * m).astype(BF16), kpb) * dec_ref[2 * p + half]
                intra = intra + _dot(s.astype(BF16), (vp * m).astype(BF16))
            st = state_ref[p]
            cross = _dot(qp.astype(BF16), st.astype(BF16)) * qd[:, ls]
            kv = _dot_tn(kdec[:, ls].astype(BF16), vp.astype(BF16))
            state_ref[p] = cd_ref[p] * st + kv * bd_ref[...]
            pairs.append(intra + cross)
        ys.append(jnp.concatenate(pairs, axis=1))
    y = jnp.concatenate(ys, axis=0)
    yn = _group_norm(y, gmat_ref[...]) * gng_ref[...]
    g = pr_ref[0, :, 3 * C_WIDTH:4 * C_WIDTH]
    oc_ref[0] = jax.nn.silu(g) * yn
    rout_ref[0] = state_ref[...]


def _ret(pr, cos, sin, dec, qd, kd, cd, bd, gn_g, gmat, r0, chunk, rows):
    nb, t, _ = pr.shape
    npair = C_HEADS // 2
    c2 = lambda b, i: (0, 0)
    c3 = lambda b, i: (0, 0, 0)
    return pl.pallas_call(
        functools.partial(_ret_kernel, chunk=chunk, rows=rows),
        grid=(nb, t // rows),
        in_specs=[pl.BlockSpec((1, rows, SEG_RET), lambda b, i: (b, i, 0)),
                  pl.BlockSpec((rows, LANE), lambda b, i: (i, 0)),
                  pl.BlockSpec((rows, LANE), lambda b, i: (i, 0)),
                  pl.BlockSpec(dec.shape, c3),
                  pl.BlockSpec(qd.shape, c2),
                  pl.BlockSpec(kd.shape, c2),
                  pl.BlockSpec(cd.shape, c3),
                  pl.BlockSpec(bd.shape, c2),
                  pl.BlockSpec((1, C_WIDTH), c2),
                  pl.BlockSpec(gmat.shape, c2),
                  pl.BlockSpec((1, npair, LANE, LANE), lambda b, i: (b, 0, 0, 0))],
        out_specs=[pl.BlockSpec((1, rows, C_WIDTH), lambda b, i: (b, i, 0)),
                   pl.BlockSpec((1, npair, LANE, LANE), lambda b, i: (b, 0, 0, 0))],
        out_shape=[jax.ShapeDtypeStruct((nb, t, C_WIDTH), F32),
                   jax.ShapeDtypeStruct((nb, npair, LANE, LANE), F32)],
        scratch_shapes=[pltpu.VMEM((npair, LANE, LANE), F32)],
        compiler_params=_params(2),
        name="ret",
    )(pr, cos, sin, dec, qd, kd, cd, bd, gn_g, gmat, r0)


def _lane_blocks(x):
    return [x[:, i * LANE:(i + 1) * LANE] for i in range(x.shape[1] // LANE)]


def _key_to_float(key):
    return pltpu.bitcast(key ^ ((key >> 31) & 0x7FFFFFFF), F32)


def _threshold_float(key):
    subnormal = jnp.logical_and(key > 0, key < MIN_NORMAL_KEY)
    return _key_to_float(jnp.where(subnormal, MIN_NORMAL_KEY, key))


def _float_to_key(x):
    bits = pltpu.bitcast(x, I32)
    return bits ^ ((bits >> 31) & 0x7FFFFFFF)


def _dsa_kernel(qb_ref, qi_ref, kw_ref, kv_ref, kx_ref, tri_ref, o_ref,
                sc_ref, wb_ref, m_ref, l_ref, acc_ref, mx_ref, mn_ref,
                *, groups, tg, nsb_total, causal, k_sel, n_keys):
    ks = KEY_SUPER
    nrep = ks // LANE
    tq = groups * tg
    if causal:
        j = pl.program_id(1)
        nsb = (j * tq) // ks + 1
        row = lax.broadcasted_iota(I32, (tq, 1), 0)
        limit = ((j * tq + row) // CHUNK + 1) * CHUNK
    else:
        nsb = nsb_total
        limit = jnp.full((tq, 1), n_keys, I32)
    lane = lax.broadcasted_iota(I32, (tq, ks), 1)

    def wide(x):
        return jnp.concatenate([x] * nrep, axis=1)

    def stack_heads(ref, g, n_heads):
        return jnp.concatenate([ref[g][:, h * LANE:(h + 1) * LANE] for h in range(n_heads)], axis=0)

    def group_rows(per_group):
        return jnp.concatenate(per_group, axis=0) if groups > 1 else per_group[0]

    qs = [stack_heads(qb_ref, g, B_HEADS) for g in range(groups)]
    qis = [stack_heads(qi_ref, g, IDX_HEADS) for g in range(groups)]

    for g in range(groups):
        w = kw_ref[g][:, IDX_DIM:IDX_DIM + IDX_HEADS] * (IDX_DIM ** -0.5 * IDX_HEADS ** -0.5)
        for h in range(IDX_HEADS):
            wb_ref[h, g * tg:(g + 1) * tg, :] = jnp.broadcast_to(w[:, h:h + 1], (tg, LANE))

    mx_ref[...] = jnp.full(mx_ref.shape, -jnp.inf, F32)
    mn_ref[...] = jnp.full(mn_ref.shape, jnp.inf, F32)

    def score_body(sb, carry):
        off = pl.multiple_of(sb * ks, ks)
        accs = []
        for g in range(groups):
            d = _dot_nt(qis[g], kx_ref[g, pl.ds(off, ks), :])
            acc = jnp.zeros((tg, ks), F32)
            for h in range(IDX_HEADS):
                acc = acc + jnp.maximum(d[h * tg:(h + 1) * tg], 0.0) * wide(wb_ref[h, g * tg:(g + 1) * tg, :])
            accs.append(acc)
        acc = group_rows(accs)
        adm = off + lane < limit
        sc = jnp.where(adm, acc, -jnp.inf)
        sc_ref[sb] = sc
        mx_ref[...] = functools.reduce(jnp.maximum, _lane_blocks(sc), mx_ref[...])
        mn_ref[...] = functools.reduce(jnp.minimum, _lane_blocks(jnp.where(adm, acc, jnp.inf)), mn_ref[...])
        return carry

    lax.fori_loop(0, nsb, score_body, 0)

    rows = min(tq, LANE)
    dense = tq % LANE == 0

    def to_state(per_row_lanes, reduce_fn):
        if dense:
            return reduce_fn(per_row_lanes.T, axis=0, keepdims=True)
        return reduce_fn(per_row_lanes, axis=1, keepdims=True)

    def to_rows(state, r0):
        if dense:
            return jnp.broadcast_to(state[:, r0:r0 + rows], (rows, LANE)).T
        return jnp.broadcast_to(state, (rows, LANE))

    def gather_state(parts):
        return jnp.concatenate(parts, axis=1 if dense else 0) if len(parts) > 1 else parts[0]

    def count(cand, strict):
        starts = range(0, tq, rows)
        cbs = [to_rows(cand, r0) for r0 in starts]
        accs = []
        for r0, cb in zip(starts, cbs):
            def body(sb, acc, r0=r0, cb=cb):
                for i in range(nrep):
                    x = sc_ref[sb, r0:r0 + rows, i * LANE:(i + 1) * LANE]
                    acc = acc + jnp.where(x > cb if strict else x >= cb, 1.0, 0.0)
                return acc

            accs.append(lax.fori_loop(0, nsb, body, jnp.zeros((rows, LANE), F32)))
        return gather_state([to_state(acc, jnp.sum) for acc in accs])

    if causal and dense:
        pos = j * tq + lax.broadcasted_iota(I32, (1, tq), 1)
        n_adm = ((pos // CHUNK + 1) * CHUNK).astype(F32)
    else:
        n_adm = jnp.minimum(limit, n_keys).astype(F32)
    row_min = gather_state([to_state(mn_ref[r0:r0 + rows], jnp.min) for r0 in range(0, tq, rows)])
    row_max = gather_state([to_state(mx_ref[r0:r0 + rows], jnp.max) for r0 in range(0, tq, rows)])
    short = n_adm <= k_sel
    lo0 = jnp.where(short, NEG_INF_KEY + 1, _float_to_key(row_min))
    hi0 = jnp.where(short, NEG_INF_KEY + 2, _float_to_key(row_max) + 1)
    c0 = jnp.where(short, float(k_sel), n_adm)

    def unresolved(lo, hi, c_lo):
        return jnp.logical_and(hi > lo + 1, c_lo != k_sel)

    def any_unresolved(lo, hi, c_lo):
        return jnp.max(jnp.where(unresolved(lo, hi, c_lo), 1.0, 0.0))

    def value_mid(lo, hi):
        return _float_to_key(0.5 * _key_to_float(lo) + 0.5 * _key_to_float(hi))

    def step(lo, hi, c_lo, cand, strict):
        live = unresolved(lo, hi, c_lo)
        cand = jnp.minimum(jnp.maximum(cand, lo + 1 - strict), hi - 1 - strict)
        cnt = count(_threshold_float(cand), bool(strict))
        edge = cand + strict
        take = jnp.logical_and(live, cnt >= k_sel)
        drop = jnp.logical_and(live, cnt < k_sel)
        return jnp.where(take, edge, lo), jnp.where(drop, edge, hi), jnp.where(take, cnt, c_lo)

    lo, hi, c_lo = step(lo0, hi0, c0, jnp.where(jnp.logical_and(lo0 < 0, hi0 > 0), 0, value_mid(lo0, hi0)), 0)
    lo, hi, c_lo = step(lo, hi, c_lo, jnp.where(lo == 0, 0, value_mid(lo, hi)), 1)

    def sel_cond(carry):
        return jnp.logical_and(carry[0] < SELECT_MAX_PASSES, carry[1] > 0.5)

    def sel_body(carry):
        it, _, lo, hi, c_lo = carry
        active = any_unresolved(lo, hi, c_lo)
        kmid = (lo >> 1) + (hi >> 1) + (lo & hi & 1)
        lo, hi, c_lo = step(lo, hi, c_lo, jnp.where(it < SELECT_VALUE_PASSES, value_mid(lo, hi), kmid), 0)
        return it + 1, active, lo, hi, c_lo

    _, _, lo, _, c_lo = lax.while_loop(
        sel_cond, sel_body, (jnp.int32(0), any_unresolved(lo, hi, c_lo), lo, hi, c_lo))
    thr = _threshold_float(lo)

    def rows_of(state):
        return jnp.concatenate([to_rows(state, r0) for r0 in range(0, tq, rows)], axis=0)

    thr_rows = wide(rows_of(thr))

    any_tied = jnp.max(jnp.where(c_lo > k_sel, 1.0, 0.0)) > 0.5

    @pl.when(any_tied)
    def _():
        need = rows_of(k_sel - count(thr, True))[:, :1]

        def body(sb, before):
            sc = sc_ref[sb]
            eq = sc == thr_rows
            eqf = jnp.where(eq, 1.0, 0.0)
            prefix = _dot(eqf.astype(BF16), tri_ref[...]) + before
            drop = jnp.logical_and(eq, prefix >= need)
            sc_ref[sb] = jnp.where(drop, -jnp.inf, sc)
            return before + jnp.sum(eqf, axis=1, keepdims=True)

        lax.fori_loop(0, nsb, body, jnp.zeros((tq, 1), F32))

    def bias_body(sb, carry):
        sc_ref[sb] = jnp.where(sc_ref[sb] >= thr_rows, 0.0, -jnp.inf)
        return carry

    lax.fori_loop(0, nsb, bias_body, 0)

    def logits(sb):
        off = pl.multiple_of(sb * ks, ks)
        kvs = [kv_ref[g, pl.ds(off, ks), :] for g in range(groups)]
        s = [_dot_nt(qs[g], kvs[g]) for g in range(groups)]
        return [group_rows([s[g][h * tg:(h + 1) * tg] for g in range(groups)]) for h in range(B_HEADS)], kvs

    key_lanes = lax.broadcasted_iota(I32, (ks, LANE), 1) < HEAD_DIM

    def knorm_body(sb, best):
        off = pl.multiple_of(sb * ks, ks)
        out = []
        for g in range(groups):
            kf = jnp.where(key_lanes, kv_ref[g, pl.ds(off, ks), :].astype(F32), 0.0)
            out.append(jnp.maximum(best[g], jnp.max(jnp.sum(kf * kf, axis=1, keepdims=True), axis=0, keepdims=True)))
        return tuple(out)

    k_sq = lax.fori_loop(0, nsb, knorm_body, tuple(jnp.zeros((1, 1), F32) for _ in range(groups)))
    worst = jnp.zeros((1, 1), F32)
    for g in range(groups):
        qf = qs[g].astype(F32)
        bound = jnp.sqrt(jnp.sum(qf * qf, axis=1, keepdims=True) * k_sq[g]) * NORM_BOUND_SLACK
        worst = jnp.maximum(worst, jnp.max(bound, axis=0, keepdims=True))
        for h in range(B_HEADS):
            m_ref[h, g * tg:(g + 1) * tg, :] = jnp.broadcast_to(bound[h * tg:(h + 1) * tg], (tg, LANE))

    @pl.when(jnp.max(worst) > SAFE_LOGIT_BOUND)
    def _():
        m_ref[...] = jnp.full(m_ref.shape, -jnp.inf, F32)

        def max_body(sb, carry):
            s, _ = logits(sb)
            bias = sc_ref[sb]
            for h in range(B_HEADS):
                m_ref[h] = functools.reduce(jnp.maximum, _lane_blocks(s[h] + bias), m_ref[h])
            return carry

        lax.fori_loop(0, nsb, max_body, 0)
        for h in range(B_HEADS):
            m_ref[h] = jnp.broadcast_to(jnp.max(m_ref[h], axis=1, keepdims=True), (tq, LANE))

    l_ref[...] = jnp.zeros(l_ref.shape, F32)
    acc_ref[...] = jnp.zeros(acc_ref.shape, F32)

    def pv_body(sb, carry):
        s, kvs = logits(sb)
        bias = sc_ref[sb]
        ps = []
        for h in range(B_HEADS):
            p = jnp.exp(s[h] + bias - wide(m_ref[h]))
            l_ref[h] += sum(_lane_blocks(p))
            ps.append(p.astype(BF16))
        for g in range(groups):
            pg = jnp.concatenate([ps[h][g * tg:(g + 1) * tg] for h in range(B_HEADS)], axis=0)
            acc_ref[g * B_HEADS * tg:(g + 1) * B_HEADS * tg, :] += _dot(pg, kvs[g])
        return carry

    lax.fori_loop(0, nsb, pv_body, 0)

    lane_head = lax.broadcasted_iota(I32, (tg, LANE), 1)
    for g in range(groups):
        outs = [acc_ref[(g * B_HEADS + h) * tg:(g * B_HEADS + h + 1) * tg, :]
                / jnp.sum(l_ref[h, g * tg:(g + 1) * tg, :], axis=1, keepdims=True) for h in range(B_HEADS)]
        for p in range(B_HEADS // 2):
            even = pltpu.roll(outs[2 * p], HEAD_DIM, axis=1)
            o_ref[g, :, p * LANE:(p + 1) * LANE] = jnp.where(lane_head < HEAD_DIM, even, outs[2 * p + 1])


def _dsa(qb, qi, kw, kvb, kxb, tri, groups, tg, causal, n_keys):
    nb, t, _ = qb.shape
    nk = kvb.shape[1]
    nsb_total = nk // KEY_SUPER
    tq = groups * tg
    assert nb % groups == 0 and t % tg == 0 and not (causal and groups > 1)
    kernel = functools.partial(_dsa_kernel, groups=groups, tg=tg, nsb_total=nsb_total, causal=causal,
                               k_sel=min(TOPK_MAX, n_keys // 4), n_keys=n_keys)
    return pl.pallas_call(
        kernel,
        grid=(nb // groups, t // tg),
        in_specs=[pl.BlockSpec((groups, tg, SEG_QB), lambda b, j: (b, j, 0)),
                  pl.BlockSpec((groups, tg, SEG_QI), lambda b, j: (b, j, 0)),
                  pl.BlockSpec((groups, tg, SEG_KW), lambda b, j: (b, j, 0)),
                  pl.BlockSpec((groups, nk, SEG_KV), lambda b, j: (b, 0, 0)),
                  pl.BlockSpec((groups, nk, SEG_KW), lambda b, j: (b, 0, 0)),
                  pl.BlockSpec(tri.shape, lambda b, j: (0, 0))],
        out_specs=pl.BlockSpec((groups, tg, B_WIDTH), lambda b, j: (b, j, 0)),
        out_shape=jax.ShapeDtypeStruct((nb, t, B_WIDTH), F32),
        scratch_shapes=[pltpu.VMEM((nsb_total, tq, KEY_SUPER), F32),
                        pltpu.VMEM((IDX_HEADS, tq, LANE), F32),
                        pltpu.VMEM((B_HEADS, tq, LANE), F32),
                        pltpu.VMEM((B_HEADS, tq, LANE), F32),
                        pltpu.VMEM((B_HEADS * tq, LANE), F32),
                        pltpu.VMEM((tq, LANE), F32),
                        pltpu.VMEM((tq, LANE), F32)],
        compiler_params=_params(2),
        name="dsa",
    )(qb, qi, kw, kvb, kxb, tri)


def _ffn_kernel(*refs, tm, alpha, streaming, seq):
    if streaming:
        (x_ref, oa_ref, ob_ref, oc_ref, wo_ref, g1_ref, b1_ref, wg_ref, wu_ref, cw_ref, cb_ref, wd_ref,
         g2_ref, b2_ref, y_ref, cs_ref, x1_ref, x1b_ref, acc_ref, carry_ref) = refs
    else:
        (x_ref, oa_ref, ob_ref, oc_ref, wo_ref, g1_ref, b1_ref, wg_ref, wu_ref, cw_ref, cb_ref, wd_ref,
         g2_ref, b2_ref, p1_ref, p2_ref, y_ref, cs_ref, x1_ref, x1b_ref, acc_ref) = refs
    f = pl.program_id(1)
    nf = pl.num_programs(1)

    @pl.when(f == 0)
    def _():
        mix = (_dot(oa_ref[...].astype(BF16), wo_ref[0:A_WIDTH, :])
               + _dot(ob_ref[...].astype(BF16), wo_ref[A_WIDTH:A_WIDTH + B_WIDTH, :])
               + _dot(oc_ref[...].astype(BF16), wo_ref[A_WIDTH + B_WIDTH:, :]))
        x1 = _layer_norm(alpha * x_ref[...] + mix, g1_ref[...], b1_ref[...])
        x1_ref[...] = x1
        x1b_ref[...] = x1.astype(BF16)
        acc_ref[...] = jnp.zeros_like(acc_ref)

    xb = x1b_ref[...]
    hg = _dot(xb, wg_ref[...])
    hu = _dot(xb, wu_ref[...])
    tf = hg.shape[1]
    h1 = pltpu.roll(hg, 1, axis=0)
    h2 = pltpu.roll(hg, 2, axis=0)
    if streaming:
        @pl.when((pl.program_id(0) * tm) % seq == 0)
        def _():
            carry_ref[f] = jnp.zeros((8, tf), F32)

        prev = carry_ref[f]
        c0 = prev[6:7, :]
        c1 = prev[7:8, :]
        top = lax.broadcasted_iota(I32, (8, tf), 0)
        h1 = jnp.concatenate([jnp.where(top == 0, c1, h1[:8]), h1[8:]], axis=0)
        h2 = jnp.concatenate([jnp.where(top == 0, c0, jnp.where(top == 1, c1, h2[:8])), h2[8:]], axis=0)
        carry_ref[f] = hg[tm - 8:, :]
        cs_ref[0] = hg[tm - 8:, :]
    else:
        pos = lax.broadcasted_iota(I32, (tm, tf), 0) % seq
        h1 = jnp.where(pos == 0, p1_ref[...], h1)
        h2 = jnp.where(pos < 2, p2_ref[...], h2)
        cs_ref[...] = hg
    cw = cw_ref[...]
    conv = cb_ref[...] + cw[0:1, :] * h2 + cw[1:2, :] * h1 + cw[2:3, :] * hg
    act = jax.nn.gelu(conv) * hu
    acc_ref[...] += _dot(act.astype(BF16), wd_ref[...])

    @pl.when(f == nf - 1)
    def _():
        y_ref[...] = _layer_norm(alpha * x1_ref[...] + acc_ref[...], g2_ref[...], b2_ref[...])


def _ffn(x2d, oa, ob, oc, lw, tm, tf, seq, prev=None):
    t = x2d.shape[0]
    nf = D_FF // tf
    streaming = prev is None
    alpha = lw["alpha"]
    row = lambda i, f: (i, 0)
    const = lambda i, f: (0, 0)
    in_specs = [pl.BlockSpec((tm, D_MODEL), row),
                pl.BlockSpec((tm, A_WIDTH), row),
                pl.BlockSpec((tm, B_WIDTH), row),
                pl.BlockSpec((tm, C_WIDTH), row),
                pl.BlockSpec((D_MODEL, D_MODEL), const),
                pl.BlockSpec((1, D_MODEL), const),
                pl.BlockSpec((1, D_MODEL), const),
                pl.BlockSpec((D_MODEL, tf), lambda i, f: (0, f)),
                pl.BlockSpec((D_MODEL, tf), lambda i, f: (0, f)),
                pl.BlockSpec((CONV_W, tf), lambda i, f: (0, f)),
                pl.BlockSpec((1, tf), lambda i, f: (0, f)),
                pl.BlockSpec((tf, D_MODEL), lambda i, f: (f, 0)),
                pl.BlockSpec((1, D_MODEL), const),
                pl.BlockSpec((1, D_MODEL), const)]
    args = [x2d, oa, ob, oc, lw["w_out"], lw["ln1_g"], lw["ln1_b"], lw["w_gate"], lw["w_up"],
            lw["conv_w"], lw["conv_b"], lw["w_down"], lw["ln2_g"], lw["ln2_b"]]
    scratch = [pltpu.VMEM((tm, D_MODEL), F32), pltpu.VMEM((tm, D_MODEL), BF16),
               pltpu.VMEM((tm, D_MODEL), F32)]
    if streaming:
        cs_spec = pl.BlockSpec((1, 8, tf), lambda i, f: (i, 0, f))
        cs_shape = jax.ShapeDtypeStruct((t // tm, 8, D_FF), F32)
        scratch.append(pltpu.VMEM((nf, 8, tf), F32))
    else:
        in_specs += [pl.BlockSpec((tm, tf), lambda i, f: (i, f))] * 2
        args += list(prev)
        cs_spec = pl.BlockSpec((tm, tf), lambda i, f: (i, f))
        cs_shape = jax.ShapeDtypeStruct((t, D_FF), F32)
    return pl.pallas_call(
        functools.partial(_ffn_kernel, tm=tm, alpha=alpha, streaming=streaming, seq=seq),
        grid=(t // tm, nf),
        in_specs=in_specs,
        out_specs=[pl.BlockSpec((tm, D_MODEL), row), cs_spec],
        out_shape=[jax.ShapeDtypeStruct((t, D_MODEL), F32), cs_shape],
        scratch_shapes=scratch,
        compiler_params=_params(2, FFN_VMEM_LIMIT),
        name="ffn",
    )(*args)


def _pad_cols(w, width):
    return jnp.pad(w, ((0, 0), (0, width - w.shape[1])))


def _proj_weight(w_in):
    sizes = (A_WIDTH, A_WIDTH, B_WIDTH, HEAD_DIM, HEAD_DIM, IDX_HEADS * IDX_DIM, IDX_DIM, IDX_HEADS,
             C_WIDTH, C_WIDTH, C_WIDTH, C_WIDTH)
    splits = np.cumsum(sizes)[:-1].tolist()
    ua, va, qb, kb, vb, qib, kib, wib, qc, kc, vc, gc = jnp.split(w_in, splits, axis=1)
    qb = qb * (HEAD_DIM ** -0.5)
    cols = [ua, va]
    cols += [_pad_cols(qb[:, h * HEAD_DIM:(h + 1) * HEAD_DIM], LANE) for h in range(B_HEADS)]
    cols += [qc, kc, vc, gc]
    cols += [_pad_cols(qib[:, h * IDX_DIM:(h + 1) * IDX_DIM], LANE) for h in range(IDX_HEADS)]
    cols += [kb, vb, _pad_cols(jnp.concatenate([kib, wib], axis=1), LANE)]
    return jnp.concatenate(cols, axis=1).astype(BF16)


def _group_matrix(width):
    g = np.arange(width) // HEAD_DIM
    return jnp.asarray((g[:, None] == g[None, :]).astype(np.float32) / HEAD_DIM, BF16)


def _rotary_tables(pos):
    half = HEAD_DIM // 2
    freqs = ROPE_BASE ** (-jnp.arange(half, dtype=F32) / half)
    ang = pos.astype(F32)[:, None] * freqs
    cos, sin = jnp.cos(ang), jnp.sin(ang)
    cos = jnp.concatenate([cos, cos, cos, cos], axis=1)
    sin = jnp.concatenate([-sin, sin, -sin, sin], axis=1)
    return cos, sin


def _retention_tables(c):
    log_g = jnp.log(1.0 - 2.0 ** (-5.0 - jnp.arange(C_HEADS, dtype=F32)))
    i = jnp.arange(c, dtype=F32)
    diff = i[:, None] - i[None, :]
    dec = jnp.where(diff >= 0, jnp.exp(jnp.maximum(diff, 0.0)[None] * log_g[:, None, None]), 0.0)
    qd = jnp.repeat(jnp.exp((i + 1)[None] * log_g[:, None]).T, HEAD_DIM, axis=1)
    kd = jnp.repeat(jnp.exp((c - 1 - i)[None] * log_g[:, None]).T, HEAD_DIM, axis=1)
    cdl = jnp.repeat(jnp.exp(c * log_g), HEAD_DIM).reshape(C_HEADS // 2, LANE)
    cd = jnp.broadcast_to(cdl[:, :, None], (C_HEADS // 2, LANE, LANE))
    blk = np.arange(LANE) // HEAD_DIM
    bd = jnp.asarray((blk[:, None] == blk[None, :]).astype(np.float32))
    return dec, qd, kd, cd, bd


def _pair_states(r):
    n = r.shape[0]
    r = r.reshape(n, C_HEADS // 2, 2, HEAD_DIM, HEAD_DIM).astype(F32)
    z = jnp.zeros_like(r[:, :, 0])
    top = jnp.concatenate([r[:, :, 0], z], axis=-1)
    bot = jnp.concatenate([z, r[:, :, 1]], axis=-1)
    return jnp.concatenate([top, bot], axis=-2)


def _unpair_states(s):
    a = s[:, :, :HEAD_DIM, :HEAD_DIM]
    b = s[:, :, HEAD_DIM:, HEAD_DIM:]
    n = s.shape[0]
    return jnp.stack([a, b], axis=2).reshape(n, C_HEADS, HEAD_DIM, HEAD_DIM)


def _mixa_tables(ws, bs, c):
    tril = jnp.tril(jnp.ones((c, c), ws.dtype))
    w = ws[:, :c, :c] * tril
    wcat = jnp.concatenate([w[g] for g in range(A_GROUPS)], axis=1).astype(BF16)
    bias = jnp.repeat(bs[:, :c].T, HEAD_DIM, axis=1)
    return wcat, bias


def _layer_weights(l, depth, w_out, ln1_g, ln1_b, w_gate, w_up, conv_w, conv_b, w_down, ln2_g, ln2_b):
    return dict(alpha=float((2 * depth) ** 0.25),
                w_out=w_out[l].astype(BF16), ln1_g=ln1_g[l][None], ln1_b=ln1_b[l][None],
                w_gate=w_gate[l].astype(BF16), w_up=w_up[l].astype(BF16), conv_w=conv_w[l],
                conv_b=conv_b[l][None], w_down=w_down[l].astype(BF16),
                ln2_g=ln2_g[l][None], ln2_b=ln2_b[l][None])


def kernel(x_prompt, x_sample, cache_b_k, cache_b_v, cache_b_kidx, state_ret, state_ffn_conv,
           w_in, a_ln_g, a_ln_b, a_ws, a_bs, c_gn_g, w_out, ln1_g, ln1_b,
           w_gate, w_up, conv_w, conv_b, w_down, ln2_g, ln2_b):
    depth = w_in.shape[0]
    bp, s, _ = x_prompt.shape
    bs_, t, _ = x_sample.shape
    past = cache_b_k.shape[2]
    l_keys = past + t
    ts = bs_ * t
    tm_p = 512
    tm_s = ts if ts <= 512 else 512
    tf_p, tf_s = D_FF // 2, 256
    assert s % tm_p == 0 and s % A_CHUNK == 0 and ts % tm_s == 0 and tm_s % t == 0 and t >= 2
    assert s % KEY_SUPER == 0 and KEY_SUPER % QUERY_BLOCK == 0
    nk_s = -(-l_keys // KEY_SUPER) * KEY_SUPER
    groups_s = max(g for g in range(1, max(LANE // t, 1) + 1) if bs_ % g == 0)

    gmat_a = _group_matrix(A_WIDTH)
    gmat_c = _group_matrix(C_WIDTH)
    tri = jnp.asarray(np.triu(np.ones((KEY_SUPER, KEY_SUPER), np.float32), 1), BF16)
    cos_p, sin_p = _rotary_tables(jnp.arange(s))
    cos_s, sin_s = _rotary_tables(past + jnp.arange(t))
    ret_p = _retention_tables(CHUNK)
    ret_s = _retention_tables(t)

    xp = x_prompt.reshape(bp * s, D_MODEL)
    xs = x_sample.reshape(ts, D_MODEL)
    outs = {k: [] for k in ("kp", "vp", "kip", "rp", "cp", "ks", "vs", "kis", "rs", "cs", "avs")}
    for l in range(depth):
        w_proj = _proj_weight(w_in[l])
        lw = _layer_weights(l, depth, w_out, ln1_g, ln1_b, w_gate, w_up, conv_w, conv_b, w_down, ln2_g, ln2_b)
        ln_g, ln_b, gn_g = a_ln_g[l][None], a_ln_b[l][None], c_gn_g[l][None]

        pa, qb, pr, qi, kv, kw, kvb, kwb = _proj(xp, w_proj, tm_p)
        wcat, bias = _mixa_tables(a_ws[l], a_bs[l], A_CHUNK)
        oa, _ = _mixa(pa, wcat, bias, ln_g, ln_b, gmat_a, A_CHUNK, 512)
        r0 = jnp.zeros((bp, C_HEADS // 2, LANE, LANE), F32)
        oc, r_p = _ret(pr.reshape(bp, s, SEG_RET), cos_p, sin_p, *ret_p, gn_g, gmat_c, r0, CHUNK, 512)
        ob = _dsa(qb.reshape(bp, s, SEG_QB), qi.reshape(bp, s, SEG_QI), kw.reshape(bp, s, SEG_KW),
                  kvb.reshape(bp, s, SEG_KV), kwb.reshape(bp, s, SEG_KW), tri, 1, QUERY_BLOCK, True, s)
        xp, c_p = _ffn(xp, oa, ob.reshape(bp * s, B_WIDTH), oc.reshape(bp * s, C_WIDTH), lw, tm_p, tf_p, s)
        kv3 = kv.reshape(bp, s, SEG_KV)
        outs["kp"].append(kv3[..., :HEAD_DIM])
        outs["vp"].append(kv3[..., HEAD_DIM:])
        outs["kip"].append(kw.reshape(bp, s, SEG_KW)[..., :IDX_DIM])
        outs["rp"].append(_unpair_states(r_p))
        outs["cp"].append(c_p[s // tm_p - 1::s // tm_p, 8 - (CONV_W - 1):, :])

        pa, qb, pr, qi, kv, kw, kvb, kwb = _proj(xs, w_proj, tm_s)
        wcat, bias = _mixa_tables(a_ws[l], a_bs[l], t)
        oa, av = _mixa(pa, wcat, bias, ln_g, ln_b, gmat_a, t, tm_s)
        r0 = _pair_states(state_ret[l])
        rows_s = t
        oc, r_s = _ret(pr.reshape(bs_, t, SEG_RET), cos_s, sin_s, *ret_s, gn_g, gmat_c, r0, t, rows_s)
        pad = nk_s - l_keys
        kv_full = jnp.concatenate(
            [jnp.concatenate([cache_b_k[l], cache_b_v[l]], axis=-1).astype(BF16),
             kvb.reshape(bs_, t, SEG_KV), jnp.zeros((bs_, pad, SEG_KV), BF16)], axis=1)
        kx_full = jnp.concatenate(
            [jnp.pad(cache_b_kidx[l], ((0, 0), (0, 0), (0, SEG_KW - IDX_DIM))).astype(BF16),
             kwb.reshape(bs_, t, SEG_KW), jnp.zeros((bs_, pad, SEG_KW), BF16)], axis=1)
        ob = _dsa(qb.reshape(bs_, t, SEG_QB), qi.reshape(bs_, t, SEG_QI), kw.reshape(bs_, t, SEG_KW),
                  kv_full, kx_full, tri, groups_s, t, False, l_keys)
        cprev = state_ffn_conv[l]
        zeros = jnp.zeros((bs_, t - 2, D_FF), F32)
        p1 = jnp.concatenate([cprev[:, 1:2], zeros, zeros[:, :1]], axis=1).reshape(ts, D_FF)
        p2 = jnp.concatenate([cprev, zeros], axis=1).reshape(ts, D_FF)
        xs, hg_s = _ffn(xs, oa, ob.reshape(ts, B_WIDTH), oc.reshape(ts, C_WIDTH), lw, tm_s, tf_s, t, (p1, p2))
        kv3 = kv.reshape(bs_, t, SEG_KV)
        outs["ks"].append(kv3[..., :HEAD_DIM])
        outs["vs"].append(kv3[..., HEAD_DIM:])
        outs["kis"].append(kw.reshape(bs_, t, SEG_KW)[..., :IDX_DIM])
        outs["rs"].append(_unpair_states(r_s))
        outs["cs"].append(hg_s.reshape(bs_, t, D_FF)[:, t - (CONV_W - 1):, :])
        outs["avs"].append(av.reshape(bs_, t, A_WIDTH))

    st = lambda k: jnp.stack(outs[k])
    return (xp.reshape(bp, s, D_MODEL), xs.reshape(bs_, t, D_MODEL),
            st("kp"), st("vp"), st("kip"), st("rp"), st("cp"),
            st("ks"), st("vs"), st("kis"), st("rs"), st("cs"), st("avs"))
```

```python
import functools

import numpy as np
import jax
import jax.numpy as jnp
from jax import lax
from jax.experimental import pallas as pl
from jax.experimental.pallas import tpu as pltpu

F32 = jnp.float32
BF16 = jnp.bfloat16
I32 = jnp.int32

D_MODEL = 1024
CHUNK = 64
HEAD_DIM = 64
A_GROUPS = 4
C_HEADS = 6
B_HEADS = 6
A_WIDTH = A_GROUPS * HEAD_DIM
B_WIDTH = B_HEADS * HEAD_DIM
C_WIDTH = C_HEADS * HEAD_DIM
A_CHUNK = 128
IDX_HEADS = 8
IDX_DIM = 32
TOPK_MAX = 256
ROPE_BASE = 10000.0
D_FF = 2816
CONV_W = 3
LN_EPS = 1e-5

LANE = 128
QUERY_BLOCK = 512
KEY_SUPER = 4 * LANE
SELECT_VALUE_PASSES = 24
SELECT_MAX_PASSES = 64
SAFE_LOGIT_BOUND = 40.0
NORM_BOUND_SLACK = 1.02
VMEM_LIMIT = 48 * 1024 * 1024
FFN_VMEM_LIMIT = 58 * 1024 * 1024

SEG_A = 2 * A_WIDTH
SEG_QB = B_HEADS * LANE
SEG_RET = 4 * C_WIDTH
SEG_QI = IDX_HEADS * LANE
SEG_KV = LANE
SEG_KW = LANE
NEG_INF_KEY = np.int32(np.uint32(0x807FFFFF).astype(np.int64) - (1 << 32))
MIN_NORMAL_KEY = np.int32(0x00800000)


def _dot(a, b):
    return jnp.dot(a, b, preferred_element_type=F32)


def _dot_nt(a, b):
    return lax.dot_general(a, b, (((1,), (1,)), ((), ())), preferred_element_type=F32)


def _dot_tn(a, b):
    return lax.dot_general(a, b, (((0,), (0,)), ((), ())), preferred_element_type=F32)


def _split_dot(x, m):
    hi = x.astype(BF16)
    lo = (x - hi.astype(F32)).astype(BF16)
    return _dot(hi, m) + _dot(lo, m)


def _group_norm(x, gmat):
    mu = _split_dot(x, gmat)
    d = x - mu
    var = _split_dot(d * d, gmat)
    return d * lax.rsqrt(var + LN_EPS)


def _layer_norm(x, g, b):
    mu = jnp.mean(x, axis=-1, keepdims=True)
    d = x - mu
    var = jnp.mean(d * d, axis=-1, keepdims=True)
    return d * lax.rsqrt(var + LN_EPS) * g + b


def _params(n_axes, vmem_limit=VMEM_LIMIT):
    return pltpu.CompilerParams(dimension_semantics=("arbitrary",) * n_axes,
                                vmem_limit_bytes=vmem_limit)


def _proj_kernel(x_ref, w_ref, a_ref, qb_ref, ret_ref, qi_ref, kv_ref, kw_ref, kvb_ref, kwb_ref):
    xb = x_ref[...].astype(BF16)
    o = 0
    a_ref[...] = _dot(xb, w_ref[:, o:o + SEG_A]); o += SEG_A
    qb_ref[...] = _dot(xb, w_ref[:, o:o + SEG_QB]).astype(BF16); o += SEG_QB
    ret_ref[...] = _dot(xb, w_ref[:, o:o + SEG_RET]); o += SEG_RET
    qi_ref[...] = _dot(xb, w_ref[:, o:o + SEG_QI]).astype(BF16); o += SEG_QI
    kv = _dot(xb, w_ref[:, o:o + SEG_KV]); o += SEG_KV
    kw = _dot(xb, w_ref[:, o:o + SEG_KW])
    kv_ref[...] = kv
    kw_ref[...] = kw
    kvb_ref[...] = kv.astype(BF16)
    kwb_ref[...] = kw.astype(BF16)


def _proj(x2d, w_proj, tm):
    t = x2d.shape[0]
    n_cols = w_proj.shape[1]
    widths = (SEG_A, SEG_QB, SEG_RET, SEG_QI, SEG_KV, SEG_KW, SEG_KV, SEG_KW)
    dtypes = (F32, BF16, F32, BF16, F32, F32, BF16, BF16)
    return pl.pallas_call(
        _proj_kernel,
        grid=(t // tm,),
        in_specs=[pl.BlockSpec((tm, D_MODEL), lambda i: (i, 0)),
                  pl.BlockSpec((D_MODEL, n_cols), lambda i: (0, 0))],
        out_specs=[pl.BlockSpec((tm, w), lambda i: (i, 0)) for w in widths],
        out_shape=[jax.ShapeDtypeStruct((t, w), d) for w, d in zip(widths, dtypes)],
        compiler_params=_params(1),
        name="proj",
    )(x2d, w_proj)


def _mixa_kernel(pa_ref, wcat_ref, bias_ref, g_ref, b_ref, gmat_ref, oa_ref, av_ref, *, chunk, rows):
    pa = pa_ref[...]
    u = jax.nn.gelu(pa[:, :A_WIDTH])
    v = jax.nn.gelu(pa[:, A_WIDTH:])
    vn = _group_norm(v, gmat_ref[...]) * g_ref[...] + b_ref[...]
    av_ref[...] = vn
    group = lax.broadcasted_iota(I32, (chunk, A_WIDTH), 1) // HEAD_DIM
    wcat = wcat_ref[...]
    bias = bias_ref[...]
    for c in range(rows // chunk):
        vc = vn[c * chunk:(c + 1) * chunk, :]
        stacked = jnp.concatenate(
            [jnp.where(group == g, vc, 0.0).astype(BF16) for g in range(A_GROUPS)], axis=0)
        s = _dot(wcat, stacked) + bias
        oa_ref[c * chunk:(c + 1) * chunk, :] = u[c * chunk:(c + 1) * chunk, :] * s


def _mixa(pa, wcat, bias, ln_g, ln_b, gmat, chunk, rows):
    t = pa.shape[0]
    const = lambda i: (0, 0)
    return pl.pallas_call(
        functools.partial(_mixa_kernel, chunk=chunk, rows=rows),
        grid=(t // rows,),
        in_specs=[pl.BlockSpec((rows, SEG_A), lambda i: (i, 0)),
                  pl.BlockSpec(wcat.shape, const),
                  pl.BlockSpec(bias.shape, const),
                  pl.BlockSpec((1, A_WIDTH), const),
                  pl.BlockSpec((1, A_WIDTH), const),
                  pl.BlockSpec(gmat.shape, const)],
        out_specs=[pl.BlockSpec((rows, A_WIDTH), lambda i: (i, 0)),
                   pl.BlockSpec((rows, A_WIDTH), lambda i: (i, 0))],
        out_shape=[jax.ShapeDtypeStruct((t, A_WIDTH), F32),
                   jax.ShapeDtypeStruct((t, A_WIDTH), F32)],
        compiler_params=_params(1),
        name="mixa",
    )(pa, wcat, bias, ln_g, ln_b, gmat)


def _swap_halves(x):
    n = x.shape[1]
    lane = lax.broadcasted_iota(I32, x.shape, 1)
    fwd = pltpu.roll(x, n - HEAD_DIM // 2, axis=1)
    bwd = pltpu.roll(x, HEAD_DIM // 2, axis=1)
    return jnp.where(lane % HEAD_DIM < HEAD_DIM // 2, fwd, bwd)


def _ret_kernel(pr_ref, cos_ref, sin_ref, dec_ref, qd_ref, kd_ref, cd_ref, bd_ref, gng_ref, gmat_ref,
                r0_ref, oc_ref, rout_ref, state_ref, *, chunk, rows):
    @pl.when(pl.program_id(1) == 0)
    def _():
        state_ref[...] = r0_ref[0]

    lane = lax.broadcasted_iota(I32, (chunk, LANE), 1)
    first = (lane < HEAD_DIM).astype(F32)
    second = 1.0 - first
    qd = qd_ref[...]
    kd = kd_ref[...]
    ys = []
    for c in range(rows // chunk):
        sl = slice(c * chunk, (c + 1) * chunk)
        blk = pr_ref[0, sl, :]
        cos = jnp.concatenate([cos_ref[sl, :]] * 3, axis=1)
        sin = jnp.concatenate([sin_ref[sl, :]] * 3, axis=1)
        q = blk[:, 0:C_WIDTH]
        k = blk[:, C_WIDTH:2 * C_WIDTH]
        v = blk[:, 2 * C_WIDTH:3 * C_WIDTH]
        q = q * cos + _swap_halves(q) * sin
        k = (k * cos + _swap_halves(k) * sin) * (HEAD_DIM ** -0.5)
        kdec = k * kd
        pairs = []
        for p in range(C_HEADS // 2):
            ls = slice(p * LANE, (p + 1) * LANE)
            qp, kp, vp = q[:, ls], k[:, ls], v[:, ls]
            kpb = kp.astype(BF16)
            intra = jnp.zeros((chunk, LANE), F32)
            for half, m in ((0, first), (1, second)):
                s = _dot_nt((qp * m).astype(BF16), kpb) * dec_ref[2 * p + half]
                intra = intra + _dot(s.astype(BF16), (vp * m).astype(BF16))
            st = state_ref[p]
            cross = _dot(qp.astype(BF16), st.astype(BF16)) * qd[:, ls]
            kv = _dot_tn(kdec[:, ls].astype(BF16), vp.astype(BF16))
            state_ref[p] = cd_ref[p] * st + kv * bd_ref[...]
            pairs.append(intra + cross)
        ys.append(jnp.concatenate(pairs, axis=1))
    y = jnp.concatenate(ys, axis=0)
    yn = _group_norm(y, gmat_ref[...]) * gng_ref[...]
    g = pr_ref[0, :, 3 * C_WIDTH:4 * C_WIDTH]
    oc_ref[0] = jax.nn.silu(g) * yn
    rout_ref[0] = state_ref[...]


def _ret(pr, cos, sin, dec, qd, kd, cd, bd, gn_g, gmat, r0, chunk, rows):
    nb, t, _ = pr.shape
    npair = C_HEADS // 2
    c2 = lambda b, i: (0, 0)
    c3 = lambda b, i: (0, 0, 0)
    return pl.pallas_call(
        functools.partial(_ret_kernel, chunk=chunk, rows=rows),
        grid=(nb, t // rows),
        in_specs=[pl.BlockSpec((1, rows, SEG_RET), lambda b, i: (b, i, 0)),
                  pl.BlockSpec((rows, LANE), lambda b, i: (i, 0)),
                  pl.BlockSpec((rows, LANE), lambda b, i: (i, 0)),
                  pl.BlockSpec(dec.shape, c3),
                  pl.BlockSpec(qd.shape, c2),
                  pl.BlockSpec(kd.shape, c2),
                  pl.BlockSpec(cd.shape, c3),
                  pl.BlockSpec(bd.shape, c2),
                  pl.BlockSpec((1, C_WIDTH), c2),
                  pl.BlockSpec(gmat.shape, c2),
                  pl.BlockSpec((1, npair, LANE, LANE), lambda b, i: (b, 0, 0, 0))],
        out_specs=[pl.BlockSpec((1, rows, C_WIDTH), lambda b, i: (b, i, 0)),
                   pl.BlockSpec((1, npair, LANE, LANE), lambda b, i: (b, 0, 0, 0))],
        out_shape=[jax.ShapeDtypeStruct((nb, t, C_WIDTH), F32),
                   jax.ShapeDtypeStruct((nb, npair, LANE, LANE), F32)],
        scratch_shapes=[pltpu.VMEM((npair, LANE, LANE), F32)],
        compiler_params=_params(2),
        name="ret",
    )(pr, cos, sin, dec, qd, kd, cd, bd, gn_g, gmat, r0)


def _lane_blocks(x):
    return [x[:, i * LANE:(i + 1) * LANE] for i in range(x.shape[1] // LANE)]


def _key_to_float(key):
    return pltpu.bitcast(key ^ ((key >> 31) & 0x7FFFFFFF), F32)


def _threshold_float(key):
    subnormal = jnp.logical_and(key > 0, key < MIN_NORMAL_KEY)
    return _key_to_float(jnp.where(subnormal, MIN_NORMAL_KEY, key))


def _float_to_key(x):
    bits = pltpu.bitcast(x, I32)
    return bits ^ ((bits >> 31) & 0x7FFFFFFF)


def _dsa_kernel(qb_ref, qi_ref, kw_ref, kv_ref, kx_ref, tri_ref, o_ref,
                sc_ref, wb_ref, m_ref, l_ref, acc_ref, mx_ref, mn_ref,
                *, groups, tg, nsb_total, causal, k_sel, n_keys):
    ks = KEY_SUPER
    nrep = ks // LANE
    tq = groups * tg
    if causal:
        j = pl.program_id(1)
        nsb = (j * tq) // ks + 1
        row = lax.broadcasted_iota(I32, (tq, 1), 0)
        limit = ((j * tq + row) // CHUNK + 1) * CHUNK
    else:
        nsb = nsb_total
        limit = jnp.full((tq, 1), n_keys, I32)
    lane = lax.broadcasted_iota(I32, (tq, ks), 1)

    def wide(x):
        return jnp.concatenate([x] * nrep, axis=1)

    def stack_heads(ref, g, n_heads):
        return jnp.concatenate([ref[g][:, h * LANE:(h + 1) * LANE] for h in range(n_heads)], axis=0)

    def group_rows(per_group):
        return jnp.concatenate(per_group, axis=0) if groups > 1 else per_group[0]

    qs = [stack_heads(qb_ref, g, B_HEADS) for g in range(groups)]
    qis = [stack_heads(qi_ref, g, IDX_HEADS) for g in range(groups)]

    for g in range(groups):
        w = kw_ref[g][:, IDX_DIM:IDX_DIM + IDX_HEADS] * (IDX_DIM ** -0.5 * IDX_HEADS ** -0.5)
        for h in range(IDX_HEADS):
            wb_ref[h, g * tg:(g + 1) * tg, :] = jnp.broadcast_to(w[:, h:h + 1], (tg, LANE))

    mx_ref[...] = jnp.full(mx_ref.shape, -jnp.inf, F32)
    mn_ref[...] = jnp.full(mn_ref.shape, jnp.inf, F32)

    def score_body(sb, carry):
        off = pl.multiple_of(sb * ks, ks)
        accs = []
        for g in range(groups):
            d = _dot_nt(qis[g], kx_ref[g, pl.ds(off, ks), :])
            acc = jnp.zeros((tg, ks), F32)
            for h in range(IDX_HEADS):
                acc = acc + jnp.maximum(d[h * tg:(h + 1) * tg], 0.0) * wide(wb_ref[h, g * tg:(g + 1) * tg, :])
            accs.append(acc)
        acc = group_rows(accs)
        adm = off + lane < limit
        sc = jnp.where(adm, acc, -jnp.inf)
        sc_ref[sb] = sc
        mx_ref[...] = functools.reduce(jnp.maximum, _lane_blocks(sc), mx_ref[...])
        mn_ref[...] = functools.reduce(jnp.minimum, _lane_blocks(jnp.where(adm, acc, jnp.inf)), mn_ref[...])
        return carry

    lax.fori_loop(0, nsb, score_body, 0)

    rows = min(tq, LANE)
    dense = tq % LANE == 0

    def to_state(per_row_lanes, reduce_fn):
        if dense:
            return reduce_fn(per_row_lanes.T, axis=0, keepdims=True)
        return reduce_fn(per_row_lanes, axis=1, keepdims=True)

    def to_rows(state, r0):
        if dense:
            return jnp.broadcast_to(state[:, r0:r0 + rows], (rows, LANE)).T
        return jnp.broadcast_to(state, (rows, LANE))

    def gather_state(parts):
        return jnp.concatenate(parts, axis=1 if dense else 0) if len(parts) > 1 else parts[0]

    def count(cand, strict, live_groups=None):
        starts = range(0, tq, rows)
        cbs = [to_rows(cand, r0) for r0 in starts]
        accs = []
        for n, (r0, cb) in enumerate(zip(starts, cbs)):
            def body(sb, acc, r0=r0, cb=cb):
                for i in range(nrep):
                    x = sc_ref[sb, r0:r0 + rows, i * LANE:(i + 1) * LANE]
                    acc = acc + jnp.where(x > cb if strict else x >= cb, 1.0, 0.0)
                return acc

            def sweep(body=body):
                return lax.fori_loop(0, nsb, body, jnp.zeros((rows, LANE), F32))

            if live_groups is None:
                accs.append(sweep())
            else:
                accs.append(lax.cond(live_groups[n] > 0.5, sweep, lambda: jnp.zeros((rows, LANE), F32)))
        return gather_state([to_state(acc, jnp.sum) for acc in accs])

    if causal and dense:
        pos = j * tq + lax.broadcasted_iota(I32, (1, tq), 1)
        n_adm = ((pos // CHUNK + 1) * CHUNK).astype(F32)
    else:
        n_adm = jnp.minimum(limit, n_keys).astype(F32)
    row_min = gather_state([to_state(mn_ref[r0:r0 + rows], jnp.min) for r0 in range(0, tq, rows)])
    row_max = gather_state([to_state(mx_ref[r0:r0 + rows], jnp.max) for r0 in range(0, tq, rows)])
    short = n_adm <= k_sel
    lo0 = jnp.where(short, NEG_INF_KEY + 1, _float_to_key(row_min))
    hi0 = jnp.where(short, NEG_INF_KEY + 2, _float_to_key(row_max) + 1)
    c0 = jnp.where(short, float(k_sel), n_adm)

    def unresolved(lo, hi, c_lo):
        return jnp.logical_and(hi > lo + 1, c_lo != k_sel)

    def unresolved_groups(lo, hi, c_lo):
        u = jnp.where(unresolved(lo, hi, c_lo), 1.0, 0.0)
        if dense:
            return [jnp.max(u[:, r0:r0 + rows]) for r0 in range(0, tq, rows)]
        return [jnp.max(u)]

    def value_mid(lo, hi):
        return _float_to_key(0.5 * _key_to_float(lo) + 0.5 * _key_to_float(hi))

    def step(lo, hi, c_lo, cand, strict, live_groups=None):
        live = unresolved(lo, hi, c_lo)
        cand = jnp.minimum(jnp.maximum(cand, lo + 1 - strict), hi - 1 - strict)
        cnt = count(_threshold_float(cand), bool(strict), live_groups)
        edge = cand + strict
        take = jnp.logical_and(live, cnt >= k_sel)
        drop = jnp.logical_and(live, cnt < k_sel)
        return jnp.where(take, edge, lo), jnp.where(drop, edge, hi), jnp.where(take, cnt, c_lo)

    lo, hi, c_lo = step(lo0, hi0, c0, jnp.where(jnp.logical_and(lo0 < 0, hi0 > 0), 0, value_mid(lo0, hi0)), 0)
    lo, hi, c_lo = step(lo, hi, c_lo, jnp.where(lo == 0, 0, value_mid(lo, hi)), 1)

    def sel_cond(carry):
        return jnp.logical_and(carry[0] < SELECT_MAX_PASSES, carry[1] > 0.5)

    def sel_body(carry):
        it, _, lo, hi, c_lo = carry
        live_groups = unresolved_groups(lo, hi, c_lo)
        active = functools.reduce(jnp.maximum, live_groups)
        kmid = (lo >> 1) + (hi >> 1) + (lo & hi & 1)
        lo, hi, c_lo = step(lo, hi, c_lo, jnp.where(it < SELECT_VALUE_PASSES, value_mid(lo, hi), kmid), 0,
                            live_groups)
        return it + 1, active, lo, hi, c_lo

    _, _, lo, _, c_lo = lax.while_loop(
        sel_cond, sel_body,
        (jnp.int32(0), functools.reduce(jnp.maximum, unresolved_groups(lo, hi, c_lo)), lo, hi, c_lo))
    thr = _threshold_float(lo)

    def rows_of(state):
        return jnp.concatenate([to_rows(state, r0) for r0 in range(0, tq, rows)], axis=0)

    thr_rows = wide(rows_of(thr))

    any_tied = jnp.max(jnp.where(c_lo > k_sel, 1.0, 0.0)) > 0.5

    @pl.when(any_tied)
    def _():
        need = rows_of(k_sel - count(thr, True))[:, :1]

        def body(sb, before):
            sc = sc_ref[sb]
            eq = sc == thr_rows
            eqf = jnp.where(eq, 1.0, 0.0)
            prefix = _dot(eqf.astype(BF16), tri_ref[...]) + before
            drop = jnp.logical_and(eq, prefix >= need)
            sc_ref[sb] = jnp.where(drop, -jnp.inf, sc)
            return before + jnp.sum(eqf, axis=1, keepdims=True)

        lax.fori_loop(0, nsb, body, jnp.zeros((tq, 1), F32))

    def bias_body(sb, carry):
        sc_ref[sb] = jnp.where(sc_ref[sb] >= thr_rows, 0.0, -jnp.inf)
        return carry

    lax.fori_loop(0, nsb, bias_body, 0)

    def logits(sb):
        off = pl.multiple_of(sb * ks, ks)
        kvs = [kv_ref[g, pl.ds(off, ks), :] for g in range(groups)]
        s = [_dot_nt(qs[g], kvs[g]) for g in range(groups)]
        return [group_rows([s[g][h * tg:(h + 1) * tg] for g in range(groups)]) for h in range(B_HEADS)], kvs

    key_lanes = lax.broadcasted_iota(I32, (ks, LANE), 1) < HEAD_DIM

    def knorm_body(sb, best):
        off = pl.multiple_of(sb * ks, ks)
        out = []
        for g in range(groups):
            kf = jnp.where(key_lanes, kv_ref[g, pl.ds(off, ks), :].astype(F32), 0.0)
            out.append(jnp.maximum(best[g], jnp.max(jnp.sum(kf * kf, axis=1, keepdims=True), axis=0, keepdims=True)))
        return tuple(out)

    k_sq = lax.fori_loop(0, nsb, knorm_body, tuple(jnp.zeros((1, 1), F32) for _ in range(groups)))
    worst = jnp.zeros((1, 1), F32)
    for g in range(groups):
        qf = qs[g].astype(F32)
        bound = jnp.sqrt(jnp.sum(qf * qf, axis=1, keepdims=True) * k_sq[g]) * NORM_BOUND_SLACK
        worst = jnp.maximum(worst, jnp.max(bound, axis=0, keepdims=True))
        for h in range(B_HEADS):
            m_ref[h, g * tg:(g + 1) * tg, :] = jnp.broadcast_to(bound[h * tg:(h + 1) * tg], (tg, LANE))

    @pl.when(jnp.max(worst) > SAFE_LOGIT_BOUND)
    def _():
        m_ref[...] = jnp.full(m_ref.shape, -jnp.inf, F32)

        def max_body(sb, carry):
            s, _ = logits(sb)
            bias = sc_ref[sb]
            for h in range(B_HEADS):
                m_ref[h] = functools.reduce(jnp.maximum, _lane_blocks(s[h] + bias), m_ref[h])
            return carry

        lax.fori_loop(0, nsb, max_body, 0)
        for h in range(B_HEADS):
            m_ref[h] = jnp.broadcast_to(jnp.max(m_ref[h], axis=1, keepdims=True), (tq, LANE))

    l_ref[...] = jnp.zeros(l_ref.shape, F32)
    acc_ref[...] = jnp.zeros(acc_ref.shape, F32)

    def pv_body(sb, carry):
        s, kvs = logits(sb)
        bias = sc_ref[sb]
        ps = []
        for h in range(B_HEADS):
            p = jnp.exp(s[h] + bias - wide(m_ref[h]))
            l_ref[h] += sum(_lane_blocks(p))
            ps.append(p.astype(BF16))
        for g in range(groups):
            pg = jnp.concatenate([ps[h][g * tg:(g + 1) * tg] for h in range(B_HEADS)], axis=0)
            acc_ref[g * B_HEADS * tg:(g + 1) * B_HEADS * tg, :] += _dot(pg, kvs[g])
        return carry

    lax.fori_loop(0, nsb, pv_body, 0)

    lane_head = lax.broadcasted_iota(I32, (tg, LANE), 1)
    for g in range(groups):
        outs = [acc_ref[(g * B_HEADS + h) * tg:(g * B_HEADS + h + 1) * tg, :]
                / jnp.sum(l_ref[h, g * tg:(g + 1) * tg, :], axis=1, keepdims=True) for h in range(B_HEADS)]
        for p in range(B_HEADS // 2):
            even = pltpu.roll(outs[2 * p], HEAD_DIM, axis=1)
            o_ref[g, :, p * LANE:(p + 1) * LANE] = jnp.where(lane_head < HEAD_DIM, even, outs[2 * p + 1])


def _dsa(qb, qi, kw, kvb, kxb, tri, groups, tg, causal, n_keys):
    nb, t, _ = qb.shape
    nk = kvb.shape[1]
    nsb_total = nk // KEY_SUPER
    tq = groups * tg
    assert nb % groups == 0 and t % tg == 0 and not (causal and groups > 1)
    kernel = functools.partial(_dsa_kernel, groups=groups, tg=tg, nsb_total=nsb_total, causal=causal,
                               k_sel=min(TOPK_MAX, n_keys // 4), n_keys=n_keys)
    return pl.pallas_call(
        kernel,
        grid=(nb // groups, t // tg),
        in_specs=[pl.BlockSpec((groups, tg, SEG_QB), lambda b, j: (b, j, 0)),
                  pl.BlockSpec((groups, tg, SEG_QI), lambda b, j: (b, j, 0)),
                  pl.BlockSpec((groups, tg, SEG_KW), lambda b, j: (b, j, 0)),
                  pl.BlockSpec((groups, nk, SEG_KV), lambda b, j: (b, 0, 0)),
                  pl.BlockSpec((groups, nk, SEG_KW), lambda b, j: (b, 0, 0)),
                  pl.BlockSpec(tri.shape, lambda b, j: (0, 0))],
        out_specs=pl.BlockSpec((groups, tg, B_WIDTH), lambda b, j: (b, j, 0)),
        out_shape=jax.ShapeDtypeStruct((nb, t, B_WIDTH), F32),
        scratch_shapes=[pltpu.VMEM((nsb_total, tq, KEY_SUPER), F32),
                        pltpu.VMEM((IDX_HEADS, tq, LANE), F32),
                        pltpu.VMEM((B_HEADS, tq, LANE), F32),
                        pltpu.VMEM((B_HEADS, tq, LANE), F32),
                        pltpu.VMEM((B_HEADS * tq, LANE), F32),
                        pltpu.VMEM((tq, LANE), F32),
                        pltpu.VMEM((tq, LANE), F32)],
        compiler_params=_params(2),
        name="dsa",
    )(qb, qi, kw, kvb, kxb, tri)


def _ffn_kernel(*refs, tm, alpha, streaming, seq):
    if streaming:
        (x_ref, oa_ref, ob_ref, oc_ref, wo_ref, g1_ref, b1_ref, wg_ref, wu_ref, cw_ref, cb_ref, wd_ref,
         g2_ref, b2_ref, y_ref, cs_ref, x1_ref, x1b_ref, acc_ref, carry_ref) = refs
    else:
        (x_ref, oa_ref, ob_ref, oc_ref, wo_ref, g1_ref, b1_ref, wg_ref, wu_ref, cw_ref, cb_ref, wd_ref,
         g2_ref, b2_ref, p1_ref, p2_ref, y_ref, cs_ref, x1_ref, x1b_ref, acc_ref) = refs
    f = pl.program_id(1)
    nf = pl.num_programs(1)

    @pl.when(f == 0)
    def _():
        mix = (_dot(oa_ref[...].astype(BF16), wo_ref[0:A_WIDTH, :])
               + _dot(ob_ref[...].astype(BF16), wo_ref[A_WIDTH:A_WIDTH + B_WIDTH, :])
               + _dot(oc_ref[...].astype(BF16), wo_ref[A_WIDTH + B_WIDTH:, :]))
        x1 = _layer_norm(alpha * x_ref[...] + mix, g1_ref[...], b1_ref[...])
        x1_ref[...] = x1
        x1b_ref[...] = x1.astype(BF16)
        acc_ref[...] = jnp.zeros_like(acc_ref)

    xb = x1b_ref[...]
    hg = _dot(xb, wg_ref[...])
    hu = _dot(xb, wu_ref[...])
    tf = hg.shape[1]
    h1 = pltpu.roll(hg, 1, axis=0)
    h2 = pltpu.roll(hg, 2, axis=0)
    if streaming:
        @pl.when((pl.program_id(0) * tm) % seq == 0)
        def _():
            carry_ref[f] = jnp.zeros((8, tf), F32)

        prev = carry_ref[f]
        c0 = prev[6:7, :]
        c1 = prev[7:8, :]
        top = lax.broadcasted_iota(I32, (8, tf), 0)
        h1 = jnp.concatenate([jnp.where(top == 0, c1, h1[:8]), h1[8:]], axis=0)
        h2 = jnp.concatenate([jnp.where(top == 0, c0, jnp.where(top == 1, c1, h2[:8])), h2[8:]], axis=0)
        carry_ref[f] = hg[tm - 8:, :]
        cs_ref[0] = hg[tm - 8:, :]
    else:
        pos = lax.broadcasted_iota(I32, (tm, tf), 0) % seq
        h1 = jnp.where(pos == 0, p1_ref[...], h1)
        h2 = jnp.where(pos < 2, p2_ref[...], h2)
        cs_ref[...] = hg
    cw = cw_ref[...]
    conv = cb_ref[...] + cw[0:1, :] * h2 + cw[1:2, :] * h1 + cw[2:3, :] * hg
    act = jax.nn.gelu(conv) * hu
    acc_ref[...] += _dot(act.astype(BF16), wd_ref[...])

    @pl.when(f == nf - 1)
    def _():
        y_ref[...] = _layer_norm(alpha * x1_ref[...] + acc_ref[...], g2_ref[...], b2_ref[...])


def _ffn(x2d, oa, ob, oc, lw, tm, tf, seq, prev=None):
    t = x2d.shape[0]
    nf = D_FF // tf
    streaming = prev is None
    alpha = lw["alpha"]
    row = lambda i, f: (i, 0)
    const = lambda i, f: (0, 0)
    in_specs = [pl.BlockSpec((tm, D_MODEL), row),
                pl.BlockSpec((tm, A_WIDTH), row),
                pl.BlockSpec((tm, B_WIDTH), row),
                pl.BlockSpec((tm, C_WIDTH), row),
                pl.BlockSpec((D_MODEL, D_MODEL), const),
                pl.BlockSpec((1, D_MODEL), const),
                pl.BlockSpec((1, D_MODEL), const),
                pl.BlockSpec((D_MODEL, tf), lambda i, f: (0, f)),
                pl.BlockSpec((D_MODEL, tf), lambda i, f: (0, f)),
                pl.BlockSpec((CONV_W, tf), lambda i, f: (0, f)),
                pl.BlockSpec((1, tf), lambda i, f: (0, f)),
                pl.BlockSpec((tf, D_MODEL), lambda i, f: (f, 0)),
                pl.BlockSpec((1, D_MODEL), const),
                pl.BlockSpec((1, D_MODEL), const)]
    args = [x2d, oa, ob, oc, lw["w_out"], lw["ln1_g"], lw["ln1_b"], lw["w_gate"], lw["w_up"],
            lw["conv_w"], lw["conv_b"], lw["w_down"], lw["ln2_g"], lw["ln2_b"]]
    scratch = [pltpu.VMEM((tm, D_MODEL), F32), pltpu.VMEM((tm, D_MODEL), BF16),
               pltpu.VMEM((tm, D_MODEL), F32)]
    if streaming:
        cs_spec = pl.BlockSpec((1, 8, tf), lambda i, f: (i, 0, f))
        cs_shape = jax.ShapeDtypeStruct((t // tm, 8, D_FF), F32)
        scratch.append(pltpu.VMEM((nf, 8, tf), F32))
    else:
        in_specs += [pl.BlockSpec((tm, tf), lambda i, f: (i, f))] * 2
        args += list(prev)
        cs_spec = pl.BlockSpec((tm, tf), lambda i, f: (i, f))
        cs_shape = jax.ShapeDtypeStruct((t, D_FF), F32)
    return pl.pallas_call(
        functools.partial(_ffn_kernel, tm=tm, alpha=alpha, streaming=streaming, seq=seq),
        grid=(t // tm, nf),
        in_specs=in_specs,
        out_specs=[pl.BlockSpec((tm, D_MODEL), row), cs_spec],
        out_shape=[jax.ShapeDtypeStruct((t, D_MODEL), F32), cs_shape],
        scratch_shapes=scratch,
        compiler_params=_params(2, FFN_VMEM_LIMIT),
        name="ffn",
    )(*args)


def _pad_cols(w, width):
    return jnp.pad(w, ((0, 0), (0, width - w.shape[1])))


def _proj_weight(w_in):
    sizes = (A_WIDTH, A_WIDTH, B_WIDTH, HEAD_DIM, HEAD_DIM, IDX_HEADS * IDX_DIM, IDX_DIM, IDX_HEADS,
             C_WIDTH, C_WIDTH, C_WIDTH, C_WIDTH)
    splits = np.cumsum(sizes)[:-1].tolist()
    ua, va, qb, kb, vb, qib, kib, wib, qc, kc, vc, gc = jnp.split(w_in, splits, axis=1)
    qb = qb * (HEAD_DIM ** -0.5)
    cols = [ua, va]
    cols += [_pad_cols(qb[:, h * HEAD_DIM:(h + 1) * HEAD_DIM], LANE) for h in range(B_HEADS)]
    cols += [qc, kc, vc, gc]
    cols += [_pad_cols(qib[:, h * IDX_DIM:(h + 1) * IDX_DIM], LANE) for h in range(IDX_HEADS)]
    cols += [kb, vb, _pad_cols(jnp.concatenate([kib, wib], axis=1), LANE)]
    return jnp.concatenate(cols, axis=1).astype(BF16)


def _group_matrix(width):
    g = np.arange(width) // HEAD_DIM
    return jnp.asarray((g[:, None] == g[None, :]).astype(np.float32) / HEAD_DIM, BF16)


def _rotary_tables(pos):
    half = HEAD_DIM // 2
    freqs = ROPE_BASE ** (-jnp.arange(half, dtype=F32) / half)
    ang = pos.astype(F32)[:, None] * freqs
    cos, sin = jnp.cos(ang), jnp.sin(ang)
    cos = jnp.concatenate([cos, cos, cos, cos], axis=1)
    sin = jnp.concatenate([-sin, sin, -sin, sin], axis=1)
    return cos, sin


def _retention_tables(c):
    log_g = jnp.log(1.0 - 2.0 ** (-5.0 - jnp.arange(C_HEADS, dtype=F32)))
    i = jnp.arange(c, dtype=F32)
    diff = i[:, None] - i[None, :]
    dec = jnp.where(diff >= 0, jnp.exp(jnp.maximum(diff, 0.0)[None] * log_g[:, None, None]), 0.0)
    qd = jnp.repeat(jnp.exp((i + 1)[None] * log_g[:, None]).T, HEAD_DIM, axis=1)
    kd = jnp.repeat(jnp.exp((c - 1 - i)[None] * log_g[:, None]).T, HEAD_DIM, axis=1)
    cdl = jnp.repeat(jnp.exp(c * log_g), HEAD_DIM).reshape(C_HEADS // 2, LANE)
    cd = jnp.broadcast_to(cdl[:, :, None], (C_HEADS // 2, LANE, LANE))
    blk = np.arange(LANE) // HEAD_DIM
    bd = jnp.asarray((blk[:, None] == blk[None, :]).astype(np.float32))
    return dec, qd, kd, cd, bd


def _pair_states(r):
    n = r.shape[0]
    r = r.reshape(n, C_HEADS // 2, 2, HEAD_DIM, HEAD_DIM).astype(F32)
    z = jnp.zeros_like(r[:, :, 0])
    top = jnp.concatenate([r[:, :, 0], z], axis=-1)
    bot = jnp.concatenate([z, r[:, :, 1]], axis=-1)
    return jnp.concatenate([top, bot], axis=-2)


def _unpair_states(s):
    a = s[:, :, :HEAD_DIM, :HEAD_DIM]
    b = s[:, :, HEAD_DIM:, HEAD_DIM:]
    n = s.shape[0]
    return jnp.stack([a, b], axis=2).reshape(n, C_HEADS, HEAD_DIM, HEAD_DIM)


def _mixa_tables(ws, bs, c):
    tril = jnp.tril(jnp.ones((c, c), ws.dtype))
    w = ws[:, :c, :c] * tril
    wcat = jnp.concatenate([w[g] for g in range(A_GROUPS)], axis=1).astype(BF16)
    bias = jnp.repeat(bs[:, :c].T, HEAD_DIM, axis=1)
    return wcat, bias


def _layer_weights(l, depth, w_out, ln1_g, ln1_b, w_gate, w_up, conv_w, conv_b, w_down, ln2_g, ln2_b):
    return dict(alpha=float((2 * depth) ** 0.25),
                w_out=w_out[l].astype(BF16), ln1_g=ln1_g[l][None], ln1_b=ln1_b[l][None],
                w_gate=w_gate[l].astype(BF16), w_up=w_up[l].astype(BF16), conv_w=conv_w[l],
                conv_b=conv_b[l][None], w_down=w_down[l].astype(BF16),
                ln2_g=ln2_g[l][None], ln2_b=ln2_b[l][None])


def kernel(x_prompt, x_sample, cache_b_k, cache_b_v, cache_b_kidx, state_ret, state_ffn_conv,
           w_in, a_ln_g, a_ln_b, a_ws, a_bs, c_gn_g, w_out, ln1_g, ln1_b,
           w_gate, w_up, conv_w, conv_b, w_down, ln2_g, ln2_b):
    depth = w_in.shape[0]
    bp, s, _ = x_prompt.shape
    bs_, t, _ = x_sample.shape
    past = cache_b_k.shape[2]
    l_keys = past + t
    ts = bs_ * t
    tm_p = 512
    tm_s = ts if ts <= 512 else 512
    tf_p, tf_s = D_FF // 2, 256
    assert s % tm_p == 0 and s % A_CHUNK == 0 and ts % tm_s == 0 and tm_s % t == 0 and t >= 2
    assert s % KEY_SUPER == 0 and KEY_SUPER % QUERY_BLOCK == 0
    nk_s = -(-l_keys // KEY_SUPER) * KEY_SUPER
    groups_s = max(g for g in range(1, max(LANE // t, 1) + 1) if bs_ % g == 0)

    gmat_a = _group_matrix(A_WIDTH)
    gmat_c = _group_matrix(C_WIDTH)
    tri = jnp.asarray(np.triu(np.ones((KEY_SUPER, KEY_SUPER), np.float32), 1), BF16)
    cos_p, sin_p = _rotary_tables(jnp.arange(s))
    cos_s, sin_s = _rotary_tables(past + jnp.arange(t))
    ret_p = _retention_tables(CHUNK)
    ret_s = _retention_tables(t)

    xp = x_prompt.reshape(bp * s, D_MODEL)
    xs = x_sample.reshape(ts, D_MODEL)
    outs = {k: [] for k in ("kp", "vp", "kip", "rp", "cp", "ks", "vs", "kis", "rs", "cs", "avs")}
    for l in range(depth):
        w_proj = _proj_weight(w_in[l])
        lw = _layer_weights(l, depth, w_out, ln1_g, ln1_b, w_gate, w_up, conv_w, conv_b, w_down, ln2_g, ln2_b)
        ln_g, ln_b, gn_g = a_ln_g[l][None], a_ln_b[l][None], c_gn_g[l][None]

        pa, qb, pr, qi, kv, kw, kvb, kwb = _proj(xp, w_proj, tm_p)
        wcat, bias = _mixa_tables(a_ws[l], a_bs[l], A_CHUNK)
        oa, _ = _mixa(pa, wcat, bias, ln_g, ln_b, gmat_a, A_CHUNK, 512)
        r0 = jnp.zeros((bp, C_HEADS // 2, LANE, LANE), F32)
        oc, r_p = _ret(pr.reshape(bp, s, SEG_RET), cos_p, sin_p, *ret_p, gn_g, gmat_c, r0, CHUNK, 512)
        ob = _dsa(qb.reshape(bp, s, SEG_QB), qi.reshape(bp, s, SEG_QI), kw.reshape(bp, s, SEG_KW),
                  kvb.reshape(bp, s, SEG_KV), kwb.reshape(bp, s, SEG_KW), tri, 1, QUERY_BLOCK, True, s)
        xp, c_p = _ffn(xp, oa, ob.reshape(bp * s, B_WIDTH), oc.reshape(bp * s, C_WIDTH), lw, tm_p, tf_p, s)
        kv3 = kv.reshape(bp, s, SEG_KV)
        outs["kp"].append(kv3[..., :HEAD_DIM])
        outs["vp"].append(kv3[..., HEAD_DIM:])
        outs["kip"].append(kw.reshape(bp, s, SEG_KW)[..., :IDX_DIM])
        outs["rp"].append(_unpair_states(r_p))
        outs["cp"].append(c_p[s // tm_p - 1::s // tm_p, 8 - (CONV_W - 1):, :])

        pa, qb, pr, qi, kv, kw, kvb, kwb = _proj(xs, w_proj, tm_s)
        wcat, bias = _mixa_tables(a_ws[l], a_bs[l], t)
        oa, av = _mixa(pa, wcat, bias, ln_g, ln_b, gmat_a, t, tm_s)
        r0 = _pair_states(state_ret[l])
        rows_s = t
        oc, r_s = _ret(pr.reshape(bs_, t, SEG_RET), cos_s, sin_s, *ret_s, gn_g, gmat_c, r0, t, rows_s)
        pad = nk_s - l_keys
        kv_full = jnp.concatenate(
            [jnp.concatenate([cache_b_k[l], cache_b_v[l]], axis=-1).astype(BF16),
             kvb.reshape(bs_, t, SEG_KV), jnp.zeros((bs_, pad, SEG_KV), BF16)], axis=1)
        kx_full = jnp.concatenate(
            [jnp.pad(cache_b_kidx[l], ((0, 0), (0, 0), (0, SEG_KW - IDX_DIM))).astype(BF16),
             kwb.reshape(bs_, t, SEG_KW), jnp.zeros((bs_, pad, SEG_KW), BF16)], axis=1)
        ob = _dsa(qb.reshape(bs_, t, SEG_QB), qi.reshape(bs_, t, SEG_QI), kw.reshape(bs_, t, SEG_KW),
                  kv_full, kx_full, tri, groups_s, t, False, l_keys)
        cprev = state_ffn_conv[l]
        zeros = jnp.zeros((bs_, t - 2, D_FF), F32)
        p1 = jnp.concatenate([cprev[:, 1:2], zeros, zeros[:, :1]], axis=1).reshape(ts, D_FF)
        p2 = jnp.concatenate([cprev, zeros], axis=1).reshape(ts, D_FF)
        xs, hg_s = _ffn(xs, oa, ob.reshape(ts, B_WIDTH), oc.reshape(ts, C_WIDTH), lw, tm_s, tf_s, t, (p1, p2))
        kv3 = kv.reshape(bs_, t, SEG_KV)
        outs["ks"].append(kv3[..., :HEAD_DIM])
        outs["vs"].append(kv3[..., HEAD_DIM:])
        outs["kis"].append(kw.reshape(bs_, t, SEG_KW)[..., :IDX_DIM])
        outs["rs"].append(_unpair_states(r_s))
        outs["cs"].append(hg_s.reshape(bs_, t, D_FF)[:, t - (CONV_W - 1):, :])
        outs["avs"].append(av.reshape(bs_, t, A_WIDTH))

    st = lambda k: jnp.stack(outs[k])
    return (xp.reshape(bp, s, D_MODEL), xs.reshape(bs_, t, D_MODEL),
            st("kp"), st("vp"), st("kip"), st("rp"), st("cp"),
            st("ks"), st("vs"), st("kis"), st("rs"), st("cs"), st("avs"))
```

```python
import functools

import numpy as np
import jax
import jax.numpy as jnp
from jax import lax
from jax.experimental import pallas as pl
from jax.experimental.pallas import tpu as pltpu

F32 = jnp.float32
BF16 = jnp.bfloat16
I32 = jnp.int32

D_MODEL = 1024
CHUNK = 64
HEAD_DIM = 64
A_GROUPS = 4
C_HEADS = 6
B_HEADS = 6
A_WIDTH = A_GROUPS * HEAD_DIM
B_WIDTH = B_HEADS * HEAD_DIM
C_WIDTH = C_HEADS * HEAD_DIM
A_CHUNK = 128
IDX_HEADS = 8
IDX_DIM = 32
TOPK_MAX = 256
ROPE_BASE = 10000.0
D_FF = 2816
CONV_W = 3
LN_EPS = 1e-5

LANE = 128
QUERY_BLOCK = 512
KEY_SUPER = 4 * LANE
SELECT_VALUE_PASSES = 24
SELECT_MAX_PASSES = 64
SAFE_LOGIT_BOUND = 40.0
NORM_BOUND_SLACK = 1.02
VMEM_LIMIT = 48 * 1024 * 1024
FFN_VMEM_LIMIT = 58 * 1024 * 1024

SEG_A = 2 * A_WIDTH
SEG_QB = B_HEADS * LANE
SEG_RET = 4 * C_WIDTH
SEG_QI = IDX_HEADS * LANE
SEG_KV = LANE
SEG_KW = LANE
NEG_INF_KEY = np.int32(np.uint32(0x807FFFFF).astype(np.int64) - (1 << 32))
MIN_NORMAL_KEY = np.int32(0x00800000)


def _dot(a, b):
    return jnp.dot(a, b, preferred_element_type=F32)


def _dot_nt(a, b):
    return lax.dot_general(a, b, (((1,), (1,)), ((), ())), preferred_element_type=F32)


def _dot_tn(a, b):
    return lax.dot_general(a, b, (((0,), (0,)), ((), ())), preferred_element_type=F32)


def _split_dot(x, m):
    hi = x.astype(BF16)
    lo = (x - hi.astype(F32)).astype(BF16)
    return _dot(hi, m) + _dot(lo, m)


def _group_norm(x, gmat):
    mu = _split_dot(x, gmat)
    d = x - mu
    var = _split_dot(d * d, gmat)
    return d * lax.rsqrt(var + LN_EPS)


def _layer_norm(x, g, b):
    mu = jnp.mean(x, axis=-1, keepdims=True)
    d = x - mu
    var = jnp.mean(d * d, axis=-1, keepdims=True)
    return d * lax.rsqrt(var + LN_EPS) * g + b


def _params(n_axes, vmem_limit=VMEM_LIMIT):
    return pltpu.CompilerParams(dimension_semantics=("arbitrary",) * n_axes,
                                vmem_limit_bytes=vmem_limit)


def _proj_kernel(x_ref, w_ref, a_ref, qb_ref, ret_ref, qi_ref, kv_ref, kw_ref, kvb_ref, kwb_ref):
    xb = x_ref[...].astype(BF16)
    o = 0
    a_ref[...] = _dot(xb, w_ref[:, o:o + SEG_A]); o += SEG_A
    qb_ref[...] = _dot(xb, w_ref[:, o:o + SEG_QB]).astype(BF16); o += SEG_QB
    ret_ref[...] = _dot(xb, w_ref[:, o:o + SEG_RET]); o += SEG_RET
    qi_ref[...] = _dot(xb, w_ref[:, o:o + SEG_QI]).astype(BF16); o += SEG_QI
    kv = _dot(xb, w_ref[:, o:o + SEG_KV]); o += SEG_KV
    kw = _dot(xb, w_ref[:, o:o + SEG_KW])
    kv_ref[...] = kv
    kw_ref[...] = kw
    kvb_ref[...] = kv.astype(BF16)
    kwb_ref[...] = kw.astype(BF16)


def _proj(x2d, w_proj, tm):
    t = x2d.shape[0]
    n_cols = w_proj.shape[1]
    widths = (SEG_A, SEG_QB, SEG_RET, SEG_QI, SEG_KV, SEG_KW, SEG_KV, SEG_KW)
    dtypes = (F32, BF16, F32, BF16, F32, F32, BF16, BF16)
    return pl.pallas_call(
        _proj_kernel,
        grid=(t // tm,),
        in_specs=[pl.BlockSpec((tm, D_MODEL), lambda i: (i, 0)),
                  pl.BlockSpec((D_MODEL, n_cols), lambda i: (0, 0))],
        out_specs=[pl.BlockSpec((tm, w), lambda i: (i, 0)) for w in widths],
        out_shape=[jax.ShapeDtypeStruct((t, w), d) for w, d in zip(widths, dtypes)],
        compiler_params=_params(1),
        name="proj",
    )(x2d, w_proj)


def _mixa_kernel(pa_ref, wcat_ref, bias_ref, g_ref, b_ref, gmat_ref, oa_ref, av_ref, *, chunk, rows):
    pa = pa_ref[...]
    u = jax.nn.gelu(pa[:, :A_WIDTH])
    v = jax.nn.gelu(pa[:, A_WIDTH:])
    vn = _group_norm(v, gmat_ref[...]) * g_ref[...] + b_ref[...]
    av_ref[...] = vn
    group = lax.broadcasted_iota(I32, (chunk, A_WIDTH), 1) // HEAD_DIM
    wcat = wcat_ref[...]
    bias = bias_ref[...]
    for c in range(rows // chunk):
        vc = vn[c * chunk:(c + 1) * chunk, :]
        stacked = jnp.concatenate(
            [jnp.where(group == g, vc, 0.0).astype(BF16) for g in range(A_GROUPS)], axis=0)
        s = _dot(wcat, stacked) + bias
        oa_ref[c * chunk:(c + 1) * chunk, :] = u[c * chunk:(c + 1) * chunk, :] * s


def _mixa(pa, wcat, bias, ln_g, ln_b, gmat, chunk, rows):
    t = pa.shape[0]
    const = lambda i: (0, 0)
    return pl.pallas_call(
        functools.partial(_mixa_kernel, chunk=chunk, rows=rows),
        grid=(t // rows,),
        in_specs=[pl.BlockSpec((rows, SEG_A), lambda i: (i, 0)),
                  pl.BlockSpec(wcat.shape, const),
                  pl.BlockSpec(bias.shape, const),
                  pl.BlockSpec((1, A_WIDTH), const),
                  pl.BlockSpec((1, A_WIDTH), const),
                  pl.BlockSpec(gmat.shape, const)],
        out_specs=[pl.BlockSpec((rows, A_WIDTH), lambda i: (i, 0)),
                   pl.BlockSpec((rows, A_WIDTH), lambda i: (i, 0))],
        out_shape=[jax.ShapeDtypeStruct((t, A_WIDTH), F32),
                   jax.ShapeDtypeStruct((t, A_WIDTH), F32)],
        compiler_params=_params(1),
        name="mixa",
    )(pa, wcat, bias, ln_g, ln_b, gmat)


def _swap_halves(x):
    n = x.shape[1]
    lane = lax.broadcasted_iota(I32, x.shape, 1)
    fwd = pltpu.roll(x, n - HEAD_DIM // 2, axis=1)
    bwd = pltpu.roll(x, HEAD_DIM // 2, axis=1)
    return jnp.where(lane % HEAD_DIM < HEAD_DIM // 2, fwd, bwd)


def _ret_kernel(pr_ref, cos_ref, sin_ref, dec_ref, qd_ref, kd_ref, cd_ref, bd_ref, gng_ref, gmat_ref,
                r0_ref, oc_ref, rout_ref, state_ref, *, chunk, rows, seqs):
    @pl.when(pl.program_id(1) == 0)
    def _():
        state_ref[...] = r0_ref[...]

    lane = lax.broadcasted_iota(I32, (chunk, LANE), 1)
    first = (lane < HEAD_DIM).astype(F32)
    second = 1.0 - first
    qd = qd_ref[...]
    kd = kd_ref[...]
    ys = [[] for _ in range(seqs)]
    for c in range(rows // chunk):
        sl = slice(c * chunk, (c + 1) * chunk)
        cos = jnp.concatenate([cos_ref[sl, :]] * 3, axis=1)
        sin = jnp.concatenate([sin_ref[sl, :]] * 3, axis=1)
        for b in range(seqs):
            blk = pr_ref[b, sl, :]
            q = blk[:, 0:C_WIDTH]
            k = blk[:, C_WIDTH:2 * C_WIDTH]
            v = blk[:, 2 * C_WIDTH:3 * C_WIDTH]
            q = q * cos + _swap_halves(q) * sin
            k = (k * cos + _swap_halves(k) * sin) * (HEAD_DIM ** -0.5)
            kdec = k * kd
            pairs = []
            for p in range(C_HEADS // 2):
                ls = slice(p * LANE, (p + 1) * LANE)
                qp, kp, vp = q[:, ls], k[:, ls], v[:, ls]
                kpb = kp.astype(BF16)
                intra = jnp.zeros((chunk, LANE), F32)
                for half, m in ((0, first), (1, second)):
                    s = _dot_nt((qp * m).astype(BF16), kpb) * dec_ref[2 * p + half]
                    intra = intra + _dot(s.astype(BF16), (vp * m).astype(BF16))
                st = state_ref[b, p]
                cross = _dot(qp.astype(BF16), st.astype(BF16)) * qd[:, ls]
                kv = _dot_tn(kdec[:, ls].astype(BF16), vp.astype(BF16))
                state_ref[b, p] = cd_ref[p] * st + kv * bd_ref[...]
                pairs.append(intra + cross)
            ys[b].append(jnp.concatenate(pairs, axis=1))
    for b in range(seqs):
        y = jnp.concatenate(ys[b], axis=0) if len(ys[b]) > 1 else ys[b][0]
        yn = _group_norm(y, gmat_ref[...]) * gng_ref[...]
        g = pr_ref[b, :, 3 * C_WIDTH:4 * C_WIDTH]
        oc_ref[b] = jax.nn.silu(g) * yn
    rout_ref[...] = state_ref[...]


def _ret(pr, cos, sin, dec, qd, kd, cd, bd, gn_g, gmat, r0, chunk, rows, seqs):
    nb, t, _ = pr.shape
    npair = C_HEADS // 2
    assert nb % seqs == 0
    c2 = lambda b, i: (0, 0)
    c3 = lambda b, i: (0, 0, 0)
    return pl.pallas_call(
        functools.partial(_ret_kernel, chunk=chunk, rows=rows, seqs=seqs),
        grid=(nb // seqs, t // rows),
        in_specs=[pl.BlockSpec((seqs, rows, SEG_RET), lambda b, i: (b, i, 0)),
                  pl.BlockSpec((rows, LANE), lambda b, i: (i, 0)),
                  pl.BlockSpec((rows, LANE), lambda b, i: (i, 0)),
                  pl.BlockSpec(dec.shape, c3),
                  pl.BlockSpec(qd.shape, c2),
                  pl.BlockSpec(kd.shape, c2),
                  pl.BlockSpec(cd.shape, c3),
                  pl.BlockSpec(bd.shape, c2),
                  pl.BlockSpec((1, C_WIDTH), c2),
                  pl.BlockSpec(gmat.shape, c2),
                  pl.BlockSpec((seqs, npair, LANE, LANE), lambda b, i: (b, 0, 0, 0))],
        out_specs=[pl.BlockSpec((seqs, rows, C_WIDTH), lambda b, i: (b, i, 0)),
                   pl.BlockSpec((seqs, npair, LANE, LANE), lambda b, i: (b, 0, 0, 0))],
        out_shape=[jax.ShapeDtypeStruct((nb, t, C_WIDTH), F32),
                   jax.ShapeDtypeStruct((nb, npair, LANE, LANE), F32)],
        scratch_shapes=[pltpu.VMEM((seqs, npair, LANE, LANE), F32)],
        compiler_params=_params(2),
        name="ret",
    )(pr, cos, sin, dec, qd, kd, cd, bd, gn_g, gmat, r0)


def _lane_blocks(x):
    return [x[:, i * LANE:(i + 1) * LANE] for i in range(x.shape[1] // LANE)]


def _key_to_float(key):
    return pltpu.bitcast(key ^ ((key >> 31) & 0x7FFFFFFF), F32)


def _threshold_float(key):
    subnormal = jnp.logical_and(key > 0, key < MIN_NORMAL_KEY)
    return _key_to_float(jnp.where(subnormal, MIN_NORMAL_KEY, key))


def _float_to_key(x):
    bits = pltpu.bitcast(x, I32)
    return bits ^ ((bits >> 31) & 0x7FFFFFFF)


def _dsa_kernel(qb_ref, qi_ref, kw_ref, kv_ref, kx_ref, tri_ref, o_ref,
                sc_ref, wb_ref, m_ref, l_ref, acc_ref, mx_ref, mn_ref,
                *, groups, tg, nsb_total, causal, k_sel, n_keys):
    ks = KEY_SUPER
    nrep = ks // LANE
    tq = groups * tg
    if causal:
        j = pl.program_id(1)
        nsb = (j * tq) // ks + 1
        row = lax.broadcasted_iota(I32, (tq, 1), 0)
        limit = ((j * tq + row) // CHUNK + 1) * CHUNK
    else:
        nsb = nsb_total
        limit = jnp.full((tq, 1), n_keys, I32)
    lane = lax.broadcasted_iota(I32, (tq, ks), 1)

    def wide(x):
        return jnp.concatenate([x] * nrep, axis=1)

    def stack_heads(ref, g, n_heads):
        return jnp.concatenate([ref[g][:, h * LANE:(h + 1) * LANE] for h in range(n_heads)], axis=0)

    def group_rows(per_group):
        return jnp.concatenate(per_group, axis=0) if groups > 1 else per_group[0]

    qs = [stack_heads(qb_ref, g, B_HEADS) for g in range(groups)]
    qis = [stack_heads(qi_ref, g, IDX_HEADS) for g in range(groups)]

    for g in range(groups):
        w = kw_ref[g][:, IDX_DIM:IDX_DIM + IDX_HEADS] * (IDX_DIM ** -0.5 * IDX_HEADS ** -0.5)
        for h in range(IDX_HEADS):
            wb_ref[h, g * tg:(g + 1) * tg, :] = jnp.broadcast_to(w[:, h:h + 1], (tg, LANE))

    mx_ref[...] = jnp.full(mx_ref.shape, -jnp.inf, F32)
    mn_ref[...] = jnp.full(mn_ref.shape, jnp.inf, F32)

    def score_body(sb, carry):
        off = pl.multiple_of(sb * ks, ks)
        accs = []
        for g in range(groups):
            d = _dot_nt(qis[g], kx_ref[g, pl.ds(off, ks), :])
            acc = jnp.zeros((tg, ks), F32)
            for h in range(IDX_HEADS):
                acc = acc + jnp.maximum(d[h * tg:(h + 1) * tg], 0.0) * wide(wb_ref[h, g * tg:(g + 1) * tg, :])
            accs.append(acc)
        acc = group_rows(accs)
        adm = off + lane < limit
        sc = jnp.where(adm, acc, -jnp.inf)
        sc_ref[sb] = sc
        mx_ref[...] = functools.reduce(jnp.maximum, _lane_blocks(sc), mx_ref[...])
        mn_ref[...] = functools.reduce(jnp.minimum, _lane_blocks(jnp.where(adm, acc, jnp.inf)), mn_ref[...])
        k_sq = []
        for g in range(groups):
            kf = jnp.where(key_lanes, kv_ref[g, pl.ds(off, ks), :].astype(F32), 0.0)
            k_sq.append(jnp.maximum(carry[g], jnp.max(jnp.sum(kf * kf, axis=1, keepdims=True), axis=0, keepdims=True)))
        return tuple(k_sq)

    key_lanes = lax.broadcasted_iota(I32, (ks, LANE), 1) < HEAD_DIM
    k_sq = lax.fori_loop(0, nsb, score_body, tuple(jnp.zeros((1, 1), F32) for _ in range(groups)))

    rows = min(tq, LANE)
    dense = tq % LANE == 0

    def to_state(per_row_lanes, reduce_fn):
        if dense:
            return reduce_fn(per_row_lanes.T, axis=0, keepdims=True)
        return reduce_fn(per_row_lanes, axis=1, keepdims=True)

    def to_rows(state, r0):
        if dense:
            return jnp.broadcast_to(state[:, r0:r0 + rows], (rows, LANE)).T
        return jnp.broadcast_to(state, (rows, LANE))

    def gather_state(parts):
        return jnp.concatenate(parts, axis=1 if dense else 0) if len(parts) > 1 else parts[0]

    def count(cand, strict, live_groups=None):
        starts = range(0, tq, rows)
        cbs = [to_rows(cand, r0) for r0 in starts]
        accs = []
        for n, (r0, cb) in enumerate(zip(starts, cbs)):
            def body(sb, acc, r0=r0, cb=cb):
                for i in range(nrep):
                    x = sc_ref[sb, r0:r0 + rows, i * LANE:(i + 1) * LANE]
                    acc = acc + jnp.where(x > cb if strict else x >= cb, 1.0, 0.0)
                return acc

            def sweep(body=body):
                return lax.fori_loop(0, nsb, body, jnp.zeros((rows, LANE), F32))

            if live_groups is None:
                accs.append(sweep())
            else:
                accs.append(lax.cond(live_groups[n] > 0.5, sweep, lambda: jnp.zeros((rows, LANE), F32)))
        return gather_state([to_state(acc, jnp.sum) for acc in accs])

    if causal and dense:
        pos = j * tq + lax.broadcasted_iota(I32, (1, tq), 1)
        n_adm = ((pos // CHUNK + 1) * CHUNK).astype(F32)
    else:
        n_adm = jnp.minimum(limit, n_keys).astype(F32)
    row_min = gather_state([to_state(mn_ref[r0:r0 + rows], jnp.min) for r0 in range(0, tq, rows)])
    row_max = gather_state([to_state(mx_ref[r0:r0 + rows], jnp.max) for r0 in range(0, tq, rows)])
    short = n_adm <= k_sel
    lo0 = jnp.where(short, NEG_INF_KEY + 1, _float_to_key(row_min))
    hi0 = jnp.where(short, NEG_INF_KEY + 2, _float_to_key(row_max) + 1)
    c0 = jnp.where(short, float(k_sel), n_adm)

    def unresolved(lo, hi, c_lo):
        return jnp.logical_and(hi > lo + 1, c_lo != k_sel)

    def unresolved_groups(lo, hi, c_lo):
        u = jnp.where(unresolved(lo, hi, c_lo), 1.0, 0.0)
        if dense:
            return [jnp.max(u[:, r0:r0 + rows]) for r0 in range(0, tq, rows)]
        return [jnp.max(u)]

    def value_mid(lo, hi):
        return _float_to_key(0.5 * _key_to_float(lo) + 0.5 * _key_to_float(hi))

    def step(lo, hi, c_lo, cand, strict, live_groups=None):
        live = unresolved(lo, hi, c_lo)
        cand = jnp.minimum(jnp.maximum(cand, lo + 1 - strict), hi - 1 - strict)
        cnt = count(_threshold_float(cand), bool(strict), live_groups)
        edge = cand + strict
        take = jnp.logical_and(live, cnt >= k_sel)
        drop = jnp.logical_and(live, cnt < k_sel)
        return jnp.where(take, edge, lo), jnp.where(drop, edge, hi), jnp.where(take, cnt, c_lo)

    lo, hi, c_lo = step(lo0, hi0, c0, jnp.where(jnp.logical_and(lo0 < 0, hi0 > 0), 0, value_mid(lo0, hi0)), 0)
    lo, hi, c_lo = step(lo, hi, c_lo, jnp.where(lo == 0, 0, value_mid(lo, hi)), 1)

    def sel_cond(carry):
        return jnp.logical_and(carry[0] < SELECT_MAX_PASSES, carry[1] > 0.5)

    def sel_body(carry):
        it, _, lo, hi, c_lo = carry
        live_groups = unresolved_groups(lo, hi, c_lo)
        active = functools.reduce(jnp.maximum, live_groups)
        kmid = (lo >> 1) + (hi >> 1) + (lo & hi & 1)
        lo, hi, c_lo = step(lo, hi, c_lo, jnp.where(it < SELECT_VALUE_PASSES, value_mid(lo, hi), kmid), 0,
                            live_groups)
        return it + 1, active, lo, hi, c_lo

    _, _, lo, _, c_lo = lax.while_loop(
        sel_cond, sel_body,
        (jnp.int32(0), functools.reduce(jnp.maximum, unresolved_groups(lo, hi, c_lo)), lo, hi, c_lo))
    thr = _threshold_float(lo)

    def rows_of(state):
        return jnp.concatenate([to_rows(state, r0) for r0 in range(0, tq, rows)], axis=0)

    thr_rows = wide(rows_of(thr))

    any_tied = jnp.max(jnp.where(c_lo > k_sel, 1.0, 0.0)) > 0.5

    @pl.when(any_tied)
    def _():
        need = rows_of(k_sel - count(thr, True))[:, :1]

        def body(sb, before):
            sc = sc_ref[sb]
            eq = sc == thr_rows
            eqf = jnp.where(eq, 1.0, 0.0)
            prefix = _dot(eqf.astype(BF16), tri_ref[...]) + before
            drop = jnp.logical_and(eq, prefix >= need)
            sc_ref[sb] = jnp.where(drop, -jnp.inf, sc)
            return before + jnp.sum(eqf, axis=1, keepdims=True)

        lax.fori_loop(0, nsb, body, jnp.zeros((tq, 1), F32))

    def bias_body(sb, carry):
        sc_ref[sb] = jnp.where(sc_ref[sb] >= thr_rows, 0.0, -jnp.inf)
        return carry

    lax.fori_loop(0, nsb, bias_body, 0)

    def logits(sb):
        off = pl.multiple_of(sb * ks, ks)
        kvs = [kv_ref[g, pl.ds(off, ks), :] for g in range(groups)]
        s = [_dot_nt(qs[g], kvs[g]) for g in range(groups)]
        return [group_rows([s[g][h * tg:(h + 1) * tg] for g in range(groups)]) for h in range(B_HEADS)], kvs

    worst = jnp.zeros((1, 1), F32)
    for g in range(groups):
        qf = qs[g].astype(F32)
        bound = jnp.sqrt(jnp.sum(qf * qf, axis=1, keepdims=True) * k_sq[g]) * NORM_BOUND_SLACK
        worst = jnp.maximum(worst, jnp.max(bound, axis=0, keepdims=True))
        for h in range(B_HEADS):
            m_ref[h, g * tg:(g + 1) * tg, :] = jnp.broadcast_to(bound[h * tg:(h + 1) * tg], (tg, LANE))

    @pl.when(jnp.max(worst) > SAFE_LOGIT_BOUND)
    def _():
        m_ref[...] = jnp.full(m_ref.shape, -jnp.inf, F32)

        def max_body(sb, carry):
            s, _ = logits(sb)
            bias = sc_ref[sb]
            for h in range(B_HEADS):
                m_ref[h] = functools.reduce(jnp.maximum, _lane_blocks(s[h] + bias), m_ref[h])
            return carry

        lax.fori_loop(0, nsb, max_body, 0)
        for h in range(B_HEADS):
            m_ref[h] = jnp.broadcast_to(jnp.max(m_ref[h], axis=1, keepdims=True), (tq, LANE))

    l_ref[...] = jnp.zeros(l_ref.shape, F32)
    acc_ref[...] = jnp.zeros(acc_ref.shape, F32)

    def pv_body(sb, carry):
        s, kvs = logits(sb)
        bias = sc_ref[sb]
        ps = []
        for h in range(B_HEADS):
            p = jnp.exp(s[h] + bias - wide(m_ref[h]))
            l_ref[h] += sum(_lane_blocks(p))
            ps.append(p.astype(BF16))
        for g in range(groups):
            pg = jnp.concatenate([ps[h][g * tg:(g + 1) * tg] for h in range(B_HEADS)], axis=0)
            acc_ref[g * B_HEADS * tg:(g + 1) * B_HEADS * tg, :] += _dot(pg, kvs[g])
        return carry

    lax.fori_loop(0, nsb, pv_body, 0)

    lane_head = lax.broadcasted_iota(I32, (tg, LANE), 1)
    for g in range(groups):
        outs = [acc_ref[(g * B_HEADS + h) * tg:(g * B_HEADS + h + 1) * tg, :]
                / jnp.sum(l_ref[h, g * tg:(g + 1) * tg, :], axis=1, keepdims=True) for h in range(B_HEADS)]
        for p in range(B_HEADS // 2):
            even = pltpu.roll(outs[2 * p], HEAD_DIM, axis=1)
            o_ref[g, :, p * LANE:(p + 1) * LANE] = jnp.where(lane_head < HEAD_DIM, even, outs[2 * p + 1])


def _dsa(qb, qi, kw, kvb, kxb, tri, groups, tg, causal, n_keys):
    nb, t, _ = qb.shape
    nk = kvb.shape[1]
    nsb_total = nk // KEY_SUPER
    tq = groups * tg
    assert nb % groups == 0 and t % tg == 0 and not (causal and groups > 1)
    kernel = functools.partial(_dsa_kernel, groups=groups, tg=tg, nsb_total=nsb_total, causal=causal,
                               k_sel=min(TOPK_MAX, n_keys // 4), n_keys=n_keys)
    return pl.pallas_call(
        kernel,
        grid=(nb // groups, t // tg),
        in_specs=[pl.BlockSpec((groups, tg, SEG_QB), lambda b, j: (b, j, 0)),
                  pl.BlockSpec((groups, tg, SEG_QI), lambda b, j: (b, j, 0)),
                  pl.BlockSpec((groups, tg, SEG_KW), lambda b, j: (b, j, 0)),
                  pl.BlockSpec((groups, nk, SEG_KV), lambda b, j: (b, 0, 0)),
                  pl.BlockSpec((groups, nk, SEG_KW), lambda b, j: (b, 0, 0)),
                  pl.BlockSpec(tri.shape, lambda b, j: (0, 0))],
        out_specs=pl.BlockSpec((groups, tg, B_WIDTH), lambda b, j: (b, j, 0)),
        out_shape=jax.ShapeDtypeStruct((nb, t, B_WIDTH), F32),
        scratch_shapes=[pltpu.VMEM((nsb_total, tq, KEY_SUPER), F32),
                        pltpu.VMEM((IDX_HEADS, tq, LANE), F32),
                        pltpu.VMEM((B_HEADS, tq, LANE), F32),
                        pltpu.VMEM((B_HEADS, tq, LANE), F32),
                        pltpu.VMEM((B_HEADS * tq, LANE), F32),
                        pltpu.VMEM((tq, LANE), F32),
                        pltpu.VMEM((tq, LANE), F32)],
        compiler_params=_params(2),
        name="dsa",
    )(qb, qi, kw, kvb, kxb, tri)


def _ffn_kernel(*refs, tm, alpha, streaming, seq):
    if streaming:
        (x_ref, oa_ref, ob_ref, oc_ref, wo_ref, g1_ref, b1_ref, wg_ref, wu_ref, cw_ref, cb_ref, wd_ref,
         g2_ref, b2_ref, y_ref, cs_ref, x1_ref, x1b_ref, acc_ref, carry_ref) = refs
    else:
        (x_ref, oa_ref, ob_ref, oc_ref, wo_ref, g1_ref, b1_ref, wg_ref, wu_ref, cw_ref, cb_ref, wd_ref,
         g2_ref, b2_ref, p1_ref, p2_ref, y_ref, cs_ref, x1_ref, x1b_ref, acc_ref) = refs
    f = pl.program_id(1)
    nf = pl.num_programs(1)

    @pl.when(f == 0)
    def _():
        mix = (_dot(oa_ref[...].astype(BF16), wo_ref[0:A_WIDTH, :])
               + _dot(ob_ref[...].astype(BF16), wo_ref[A_WIDTH:A_WIDTH + B_WIDTH, :])
               + _dot(oc_ref[...].astype(BF16), wo_ref[A_WIDTH + B_WIDTH:, :]))
        x1 = _layer_norm(alpha * x_ref[...] + mix, g1_ref[...], b1_ref[...])
        x1_ref[...] = x1
        x1b_ref[...] = x1.astype(BF16)
        acc_ref[...] = jnp.zeros_like(acc_ref)

    xb = x1b_ref[...]
    hg = _dot(xb, wg_ref[...])
    hu = _dot(xb, wu_ref[...])
    tf = hg.shape[1]
    h1 = pltpu.roll(hg, 1, axis=0)
    h2 = pltpu.roll(hg, 2, axis=0)
    if streaming:
        @pl.when((pl.program_id(0) * tm) % seq == 0)
        def _():
            carry_ref[f] = jnp.zeros((8, tf), F32)

        prev = carry_ref[f]
        c0 = prev[6:7, :]
        c1 = prev[7:8, :]
        top = lax.broadcasted_iota(I32, (8, tf), 0)
        h1 = jnp.concatenate([jnp.where(top == 0, c1, h1[:8]), h1[8:]], axis=0)
        h2 = jnp.concatenate([jnp.where(top == 0, c0, jnp.where(top == 1, c1, h2[:8])), h2[8:]], axis=0)
        carry_ref[f] = hg[tm - 8:, :]
        cs_ref[0] = hg[tm - 8:, :]
    else:
        pos = lax.broadcasted_iota(I32, (tm, tf), 0) % seq
        h1 = jnp.where(pos == 0, p1_ref[...], h1)
        h2 = jnp.where(pos < 2, p2_ref[...], h2)
        cs_ref[...] = hg
    cw = cw_ref[...]
    conv = cb_ref[...] + cw[0:1, :] * h2 + cw[1:2, :] * h1 + cw[2:3, :] * hg
    act = jax.nn.gelu(conv) * hu
    acc_ref[...] += _dot(act.astype(BF16), wd_ref[...])

    @pl.when(f == nf - 1)
    def _():
        y_ref[...] = _layer_norm(alpha * x1_ref[...] + acc_ref[...], g2_ref[...], b2_ref[...])


def _ffn(x2d, oa, ob, oc, lw, tm, tf, seq, prev=None):
    t = x2d.shape[0]
    nf = D_FF // tf
    streaming = prev is None
    alpha = lw["alpha"]
    row = lambda i, f: (i, 0)
    const = lambda i, f: (0, 0)
    in_specs = [pl.BlockSpec((tm, D_MODEL), row),
                pl.BlockSpec((tm, A_WIDTH), row),
                pl.BlockSpec((tm, B_WIDTH), row),
                pl.BlockSpec((tm, C_WIDTH), row),
                pl.BlockSpec((D_MODEL, D_MODEL), const),
                pl.BlockSpec((1, D_MODEL), const),
                pl.BlockSpec((1, D_MODEL), const),
                pl.BlockSpec((D_MODEL, tf), lambda i, f: (0, f)),
                pl.BlockSpec((D_MODEL, tf), lambda i, f: (0, f)),
                pl.BlockSpec((CONV_W, tf), lambda i, f: (0, f)),
                pl.BlockSpec((1, tf), lambda i, f: (0, f)),
                pl.BlockSpec((tf, D_MODEL), lambda i, f: (f, 0)),
                pl.BlockSpec((1, D_MODEL), const),
                pl.BlockSpec((1, D_MODEL), const)]
    args = [x2d, oa, ob, oc, lw["w_out"], lw["ln1_g"], lw["ln1_b"], lw["w_gate"], lw["w_up"],
            lw["conv_w"], lw["conv_b"], lw["w_down"], lw["ln2_g"], lw["ln2_b"]]
    scratch = [pltpu.VMEM((tm, D_MODEL), F32), pltpu.VMEM((tm, D_MODEL), BF16),
               pltpu.VMEM((tm, D_MODEL), F32)]
    if streaming:
        cs_spec = pl.BlockSpec((1, 8, tf), lambda i, f: (i, 0, f))
        cs_shape = jax.ShapeDtypeStruct((t // tm, 8, D_FF), F32)
        scratch.append(pltpu.VMEM((nf, 8, tf), F32))
    else:
        in_specs += [pl.BlockSpec((tm, tf), lambda i, f: (i, f))] * 2
        args += list(prev)
        cs_spec = pl.BlockSpec((tm, tf), lambda i, f: (i, f))
        cs_shape = jax.ShapeDtypeStruct((t, D_FF), F32)
    return pl.pallas_call(
        functools.partial(_ffn_kernel, tm=tm, alpha=alpha, streaming=streaming, seq=seq),
        grid=(t // tm, nf),
        in_specs=in_specs,
        out_specs=[pl.BlockSpec((tm, D_MODEL), row), cs_spec],
        out_shape=[jax.ShapeDtypeStruct((t, D_MODEL), F32), cs_shape],
        scratch_shapes=scratch,
        compiler_params=_params(2, FFN_VMEM_LIMIT),
        name="ffn",
    )(*args)


def _pad_cols(w, width):
    return jnp.pad(w, ((0, 0), (0, width - w.shape[1])))


def _proj_weight(w_in):
    sizes = (A_WIDTH, A_WIDTH, B_WIDTH, HEAD_DIM, HEAD_DIM, IDX_HEADS * IDX_DIM, IDX_DIM, IDX_HEADS,
             C_WIDTH, C_WIDTH, C_WIDTH, C_WIDTH)
    splits = np.cumsum(sizes)[:-1].tolist()
    ua, va, qb, kb, vb, qib, kib, wib, qc, kc, vc, gc = jnp.split(w_in, splits, axis=1)
    qb = qb * (HEAD_DIM ** -0.5)
    cols = [ua, va]
    cols += [_pad_cols(qb[:, h * HEAD_DIM:(h + 1) * HEAD_DIM], LANE) for h in range(B_HEADS)]
    cols += [qc, kc, vc, gc]
    cols += [_pad_cols(qib[:, h * IDX_DIM:(h + 1) * IDX_DIM], LANE) for h in range(IDX_HEADS)]
    cols += [kb, vb, _pad_cols(jnp.concatenate([kib, wib], axis=1), LANE)]
    return jnp.concatenate(cols, axis=1).astype(BF16)


def _group_matrix(width):
    g = np.arange(width) // HEAD_DIM
    return jnp.asarray((g[:, None] == g[None, :]).astype(np.float32) / HEAD_DIM, BF16)


def _rotary_tables(pos):
    half = HEAD_DIM // 2
    freqs = ROPE_BASE ** (-jnp.arange(half, dtype=F32) / half)
    ang = pos.astype(F32)[:, None] * freqs
    cos, sin = jnp.cos(ang), jnp.sin(ang)
    cos = jnp.concatenate([cos, cos, cos, cos], axis=1)
    sin = jnp.concatenate([-sin, sin, -sin, sin], axis=1)
    return cos, sin


def _retention_tables(c):
    log_g = jnp.log(1.0 - 2.0 ** (-5.0 - jnp.arange(C_HEADS, dtype=F32)))
    i = jnp.arange(c, dtype=F32)
    diff = i[:, None] - i[None, :]
    dec = jnp.where(diff >= 0, jnp.exp(jnp.maximum(diff, 0.0)[None] * log_g[:, None, None]), 0.0)
    qd = jnp.repeat(jnp.exp((i + 1)[None] * log_g[:, None]).T, HEAD_DIM, axis=1)
    kd = jnp.repeat(jnp.exp((c - 1 - i)[None] * log_g[:, None]).T, HEAD_DIM, axis=1)
    cdl = jnp.repeat(jnp.exp(c * log_g), HEAD_DIM).reshape(C_HEADS // 2, LANE)
    cd = jnp.broadcast_to(cdl[:, :, None], (C_HEADS // 2, LANE, LANE))
    blk = np.arange(LANE) // HEAD_DIM
    bd = jnp.asarray((blk[:, None] == blk[None, :]).astype(np.float32))
    return dec, qd, kd, cd, bd


def _pair_states(r):
    n = r.shape[0]
    r = r.reshape(n, C_HEADS // 2, 2, HEAD_DIM, HEAD_DIM).astype(F32)
    z = jnp.zeros_like(r[:, :, 0])
    top = jnp.concatenate([r[:, :, 0], z], axis=-1)
    bot = jnp.concatenate([z, r[:, :, 1]], axis=-1)
    return jnp.concatenate([top, bot], axis=-2)


def _unpair_states(s):
    a = s[:, :, :HEAD_DIM, :HEAD_DIM]
    b = s[:, :, HEAD_DIM:, HEAD_DIM:]
    n = s.shape[0]
    return jnp.stack([a, b], axis=2).reshape(n, C_HEADS, HEAD_DIM, HEAD_DIM)


def _mixa_tables(ws, bs, c):
    tril = jnp.tril(jnp.ones((c, c), ws.dtype))
    w = ws[:, :c, :c] * tril
    wcat = jnp.concatenate([w[g] for g in range(A_GROUPS)], axis=1).astype(BF16)
    bias = jnp.repeat(bs[:, :c].T, HEAD_DIM, axis=1)
    return wcat, bias


def _layer_weights(l, depth, w_out, ln1_g, ln1_b, w_gate, w_up, conv_w, conv_b, w_down, ln2_g, ln2_b):
    return dict(alpha=float((2 * depth) ** 0.25),
                w_out=w_out[l].astype(BF16), ln1_g=ln1_g[l][None], ln1_b=ln1_b[l][None],
                w_gate=w_gate[l].astype(BF16), w_up=w_up[l].astype(BF16), conv_w=conv_w[l],
                conv_b=conv_b[l][None], w_down=w_down[l].astype(BF16),
                ln2_g=ln2_g[l][None], ln2_b=ln2_b[l][None])


def kernel(x_prompt, x_sample, cache_b_k, cache_b_v, cache_b_kidx, state_ret, state_ffn_conv,
           w_in, a_ln_g, a_ln_b, a_ws, a_bs, c_gn_g, w_out, ln1_g, ln1_b,
           w_gate, w_up, conv_w, conv_b, w_down, ln2_g, ln2_b):
    depth = w_in.shape[0]
    bp, s, _ = x_prompt.shape
    bs_, t, _ = x_sample.shape
    past = cache_b_k.shape[2]
    l_keys = past + t
    ts = bs_ * t
    tm_p = 512
    tm_s = ts if ts <= 512 else 512
    tf_p, tf_s = D_FF // 2, 256
    assert s % tm_p == 0 and s % A_CHUNK == 0 and ts % tm_s == 0 and tm_s % t == 0 and t >= 2
    assert s % KEY_SUPER == 0 and KEY_SUPER % QUERY_BLOCK == 0
    nk_s = -(-l_keys // KEY_SUPER) * KEY_SUPER
    groups_s = max(g for g in range(1, max(LANE // t, 1) + 1) if bs_ % g == 0)

    gmat_a = _group_matrix(A_WIDTH)
    gmat_c = _group_matrix(C_WIDTH)
    tri = jnp.asarray(np.triu(np.ones((KEY_SUPER, KEY_SUPER), np.float32), 1), BF16)
    cos_p, sin_p = _rotary_tables(jnp.arange(s))
    cos_s, sin_s = _rotary_tables(past + jnp.arange(t))
    ret_p = _retention_tables(CHUNK)
    ret_s = _retention_tables(t)

    xp = x_prompt.reshape(bp * s, D_MODEL)
    xs = x_sample.reshape(ts, D_MODEL)
    outs = {k: [] for k in ("kp", "vp", "kip", "rp", "cp", "ks", "vs", "kis", "rs", "cs", "avs")}
    for l in range(depth):
        w_proj = _proj_weight(w_in[l])
        lw = _layer_weights(l, depth, w_out, ln1_g, ln1_b, w_gate, w_up, conv_w, conv_b, w_down, ln2_g, ln2_b)
        ln_g, ln_b, gn_g = a_ln_g[l][None], a_ln_b[l][None], c_gn_g[l][None]

        pa, qb, pr, qi, kv, kw, kvb, kwb = _proj(xp, w_proj, tm_p)
        wcat, bias = _mixa_tables(a_ws[l], a_bs[l], A_CHUNK)
        oa, _ = _mixa(pa, wcat, bias, ln_g, ln_b, gmat_a, A_CHUNK, 512)
        r0 = jnp.zeros((bp, C_HEADS // 2, LANE, LANE), F32)
        oc, r_p = _ret(pr.reshape(bp, s, SEG_RET), cos_p, sin_p, *ret_p, gn_g, gmat_c, r0, CHUNK, 512,
                          2 if bp % 2 == 0 else 1)
        ob = _dsa(qb.reshape(bp, s, SEG_QB), qi.reshape(bp, s, SEG_QI), kw.reshape(bp, s, SEG_KW),
                  kvb.reshape(bp, s, SEG_KV), kwb.reshape(bp, s, SEG_KW), tri, 1, QUERY_BLOCK, True, s)
        xp, c_p = _ffn(xp, oa, ob.reshape(bp * s, B_WIDTH), oc.reshape(bp * s, C_WIDTH), lw, tm_p, tf_p, s)
        kv3 = kv.reshape(bp, s, SEG_KV)
        outs["kp"].append(kv3[..., :HEAD_DIM])
        outs["vp"].append(kv3[..., HEAD_DIM:])
        outs["kip"].append(kw.reshape(bp, s, SEG_KW)[..., :IDX_DIM])
        outs["rp"].append(_unpair_states(r_p))
        outs["cp"].append(c_p[s // tm_p - 1::s // tm_p, 8 - (CONV_W - 1):, :])

        pa, qb, pr, qi, kv, kw, kvb, kwb = _proj(xs, w_proj, tm_s)
        wcat, bias = _mixa_tables(a_ws[l], a_bs[l], t)
        oa, av = _mixa(pa, wcat, bias, ln_g, ln_b, gmat_a, t, tm_s)
        r0 = _pair_states(state_ret[l])
        oc, r_s = _ret(pr.reshape(bs_, t, SEG_RET), cos_s, sin_s, *ret_s, gn_g, gmat_c, r0, t, t, groups_s)
        pad = nk_s - l_keys
        kv_full = jnp.concatenate(
            [jnp.concatenate([cache_b_k[l], cache_b_v[l]], axis=-1).astype(BF16),
             kvb.reshape(bs_, t, SEG_KV), jnp.zeros((bs_, pad, SEG_KV), BF16)], axis=1)
        kx_full = jnp.concatenate(
            [jnp.pad(cache_b_kidx[l], ((0, 0), (0, 0), (0, SEG_KW - IDX_DIM))).astype(BF16),
             kwb.reshape(bs_, t, SEG_KW), jnp.zeros((bs_, pad, SEG_KW), BF16)], axis=1)
        ob = _dsa(qb.reshape(bs_, t, SEG_QB), qi.reshape(bs_, t, SEG_QI), kw.reshape(bs_, t, SEG_KW),
                  kv_full, kx_full, tri, groups_s, t, False, l_keys)
        cprev = state_ffn_conv[l]
        zeros = jnp.zeros((bs_, t - 2, D_FF), F32)
        p1 = jnp.concatenate([cprev[:, 1:2], zeros, zeros[:, :1]], axis=1).reshape(ts, D_FF)
        p2 = jnp.concatenate([cprev, zeros], axis=1).reshape(ts, D_FF)
        xs, hg_s = _ffn(xs, oa, ob.reshape(ts, B_WIDTH), oc.reshape(ts, C_WIDTH), lw, tm_s, tf_s, t, (p1, p2))
        kv3 = kv.reshape(bs_, t, SEG_KV)
        outs["ks"].append(kv3[..., :HEAD_DIM])
        outs["vs"].append(kv3[..., HEAD_DIM:])
        outs["kis"].append(kw.reshape(bs_, t, SEG_KW)[..., :IDX_DIM])
        outs["rs"].append(_unpair_states(r_s))
        outs["cs"].append(hg_s.reshape(bs_, t, D_FF)[:, t - (CONV_W - 1):, :])
        outs["avs"].append(av.reshape(bs_, t, A_WIDTH))

    st = lambda k: jnp.stack(outs[k])
    return (xp.reshape(bp, s, D_MODEL), xs.reshape(bs_, t, D_MODEL),
            st("kp"), st("vp"), st("kip"), st("rp"), st("cp"),
            st("ks"), st("vs"), st("kis"), st("rs"), st("cs"), st("avs"))
```

```python
import functools

import numpy as np
import jax
import jax.numpy as jnp
from jax import lax
from jax.experimental import pallas as pl
from jax.experimental.pallas import tpu as pltpu

F32 = jnp.float32
BF16 = jnp.bfloat16
I32 = jnp.int32

D_MODEL = 1024
CHUNK = 64
HEAD_DIM = 64
A_GROUPS = 4
C_HEADS = 6
B_HEADS = 6
A_WIDTH = A_GROUPS * HEAD_DIM
B_WIDTH = B_HEADS * HEAD_DIM
C_WIDTH = C_HEADS * HEAD_DIM
A_CHUNK = 128
IDX_HEADS = 8
IDX_DIM = 32
TOPK_MAX = 256
ROPE_BASE = 10000.0
D_FF = 2816
CONV_W = 3
LN_EPS = 1e-5

LANE = 128
QUERY_BLOCK = 512
KEY_SUPER = 4 * LANE
SELECT_VALUE_PASSES = 24
SELECT_MAX_PASSES = 64
SAFE_LOGIT_BOUND = 40.0
NORM_BOUND_SLACK = 1.02
VMEM_LIMIT = 48 * 1024 * 1024
FFN_VMEM_LIMIT = 58 * 1024 * 1024

SEG_A = 2 * A_WIDTH
SEG_QB = B_HEADS * LANE
SEG_RET = 4 * C_WIDTH
SEG_QI = IDX_HEADS * LANE
SEG_KV = LANE
SEG_KW = LANE
NEG_INF_KEY = np.int32(np.uint32(0x807FFFFF).astype(np.int64) - (1 << 32))
MIN_NORMAL_KEY = np.int32(0x00800000)


def _dot(a, b):
    return jnp.dot(a, b, preferred_element_type=F32)


def _dot_nt(a, b):
    return lax.dot_general(a, b, (((1,), (1,)), ((), ())), preferred_element_type=F32)


def _dot_tn(a, b):
    return lax.dot_general(a, b, (((0,), (0,)), ((), ())), preferred_element_type=F32)


def _split_dot(x, m):
    hi = x.astype(BF16)
    lo = (x - hi.astype(F32)).astype(BF16)
    return _dot(hi, m) + _dot(lo, m)


def _group_norm(x, gmat):
    mu = _split_dot(x, gmat)
    d = x - mu
    var = _split_dot(d * d, gmat)
    return d * lax.rsqrt(var + LN_EPS)


def _layer_norm(x, g, b):
    mu = jnp.mean(x, axis=-1, keepdims=True)
    d = x - mu
    var = jnp.mean(d * d, axis=-1, keepdims=True)
    return d * lax.rsqrt(var + LN_EPS) * g + b


def _params(n_axes, vmem_limit=VMEM_LIMIT):
    return pltpu.CompilerParams(dimension_semantics=("arbitrary",) * n_axes,
                                vmem_limit_bytes=vmem_limit)


def _proj_kernel(x_ref, w_ref, a_ref, qb_ref, ret_ref, qi_ref, kv_ref, kw_ref, kvb_ref, kwb_ref):
    xb = x_ref[...].astype(BF16)
    o = 0
    a_ref[...] = _dot(xb, w_ref[:, o:o + SEG_A]); o += SEG_A
    qb_ref[...] = _dot(xb, w_ref[:, o:o + SEG_QB]).astype(BF16); o += SEG_QB
    ret_ref[...] = _dot(xb, w_ref[:, o:o + SEG_RET]); o += SEG_RET
    qi_ref[...] = _dot(xb, w_ref[:, o:o + SEG_QI]).astype(BF16); o += SEG_QI
    kv = _dot(xb, w_ref[:, o:o + SEG_KV]); o += SEG_KV
    kw = _dot(xb, w_ref[:, o:o + SEG_KW])
    kv_ref[...] = kv
    kw_ref[...] = kw
    kvb_ref[...] = kv.astype(BF16)
    kwb_ref[...] = kw.astype(BF16)


def _proj(x2d, w_proj, tm):
    t = x2d.shape[0]
    n_cols = w_proj.shape[1]
    widths = (SEG_A, SEG_QB, SEG_RET, SEG_QI, SEG_KV, SEG_KW, SEG_KV, SEG_KW)
    dtypes = (F32, BF16, F32, BF16, F32, F32, BF16, BF16)
    return pl.pallas_call(
        _proj_kernel,
        grid=(t // tm,),
        in_specs=[pl.BlockSpec((tm, D_MODEL), lambda i: (i, 0)),
                  pl.BlockSpec((D_MODEL, n_cols), lambda i: (0, 0))],
        out_specs=[pl.BlockSpec((tm, w), lambda i: (i, 0)) for w in widths],
        out_shape=[jax.ShapeDtypeStruct((t, w), d) for w, d in zip(widths, dtypes)],
        compiler_params=_params(1),
        name="proj",
    )(x2d, w_proj)


def _mixa_kernel(pa_ref, wcat_ref, bias_ref, g_ref, b_ref, gmat_ref, oa_ref, av_ref, *, chunk, rows):
    pa = pa_ref[...]
    u = jax.nn.gelu(pa[:, :A_WIDTH])
    v = jax.nn.gelu(pa[:, A_WIDTH:])
    vn = _group_norm(v, gmat_ref[...]) * g_ref[...] + b_ref[...]
    av_ref[...] = vn
    group = lax.broadcasted_iota(I32, (chunk, A_WIDTH), 1) // HEAD_DIM
    wcat = wcat_ref[...]
    bias = bias_ref[...]
    for c in range(rows // chunk):
        vc = vn[c * chunk:(c + 1) * chunk, :]
        stacked = jnp.concatenate(
            [jnp.where(group == g, vc, 0.0).astype(BF16) for g in range(A_GROUPS)], axis=0)
        s = _dot(wcat, stacked) + bias
        oa_ref[c * chunk:(c + 1) * chunk, :] = u[c * chunk:(c + 1) * chunk, :] * s


def _mixa(pa, wcat, bias, ln_g, ln_b, gmat, chunk, rows):
    t = pa.shape[0]
    const = lambda i: (0, 0)
    return pl.pallas_call(
        functools.partial(_mixa_kernel, chunk=chunk, rows=rows),
        grid=(t // rows,),
        in_specs=[pl.BlockSpec((rows, SEG_A), lambda i: (i, 0)),
                  pl.BlockSpec(wcat.shape, const),
                  pl.BlockSpec(bias.shape, const),
                  pl.BlockSpec((1, A_WIDTH), const),
                  pl.BlockSpec((1, A_WIDTH), const),
                  pl.BlockSpec(gmat.shape, const)],
        out_specs=[pl.BlockSpec((rows, A_WIDTH), lambda i: (i, 0)),
                   pl.BlockSpec((rows, A_WIDTH), lambda i: (i, 0))],
        out_shape=[jax.ShapeDtypeStruct((t, A_WIDTH), F32),
                   jax.ShapeDtypeStruct((t, A_WIDTH), F32)],
        compiler_params=_params(1),
        name="mixa",
    )(pa, wcat, bias, ln_g, ln_b, gmat)


def _swap_halves(x):
    n = x.shape[1]
    lane = lax.broadcasted_iota(I32, x.shape, 1)
    fwd = pltpu.roll(x, n - HEAD_DIM // 2, axis=1)
    bwd = pltpu.roll(x, HEAD_DIM // 2, axis=1)
    return jnp.where(lane % HEAD_DIM < HEAD_DIM // 2, fwd, bwd)


def _ret_kernel(pr_ref, cos_ref, sin_ref, dec_ref, qd_ref, kd_ref, cd_ref, bd_ref, gng_ref, gmat_ref,
                r0_ref, oc_ref, rout_ref, state_ref, *, chunk, rows, seqs):
    @pl.when(pl.program_id(1) == 0)
    def _():
        state_ref[...] = r0_ref[...]

    lane = lax.broadcasted_iota(I32, (chunk, LANE), 1)
    first = (lane < HEAD_DIM).astype(F32)
    second = 1.0 - first
    qd = qd_ref[...]
    kd = kd_ref[...]
    ys = [[] for _ in range(seqs)]
    for c in range(rows // chunk):
        sl = slice(c * chunk, (c + 1) * chunk)
        cos = jnp.concatenate([cos_ref[sl, :]] * 3, axis=1)
        sin = jnp.concatenate([sin_ref[sl, :]] * 3, axis=1)
        for b in range(seqs):
            blk = pr_ref[b, sl, :]
            q = blk[:, 0:C_WIDTH]
            k = blk[:, C_WIDTH:2 * C_WIDTH]
            v = blk[:, 2 * C_WIDTH:3 * C_WIDTH]
            q = q * cos + _swap_halves(q) * sin
            k = (k * cos + _swap_halves(k) * sin) * (HEAD_DIM ** -0.5)
            kdec = k * kd
            pairs = []
            for p in range(C_HEADS // 2):
                ls = slice(p * LANE, (p + 1) * LANE)
                qp, kp, vp = q[:, ls], k[:, ls], v[:, ls]
                kpb = kp.astype(BF16)
                intra = jnp.zeros((chunk, LANE), F32)
                for half, m in ((0, first), (1, second)):
                    s = _dot_nt((qp * m).astype(BF16), kpb) * dec_ref[2 * p + half]
                    intra = intra + _dot(s.astype(BF16), (vp * m).astype(BF16))
                st = state_ref[b, p]
                cross = _dot(qp.astype(BF16), st.astype(BF16)) * qd[:, ls]
                kv = _dot_tn(kdec[:, ls].astype(BF16), vp.astype(BF16))
                state_ref[b, p] = cd_ref[p] * st + kv * bd_ref[...]
                pairs.append(intra + cross)
            ys[b].append(jnp.concatenate(pairs, axis=1))
    for b in range(seqs):
        y = jnp.concatenate(ys[b], axis=0) if len(ys[b]) > 1 else ys[b][0]
        yn = _group_norm(y, gmat_ref[...]) * gng_ref[...]
        g = pr_ref[b, :, 3 * C_WIDTH:4 * C_WIDTH]
        oc_ref[b] = jax.nn.silu(g) * yn
    rout_ref[...] = state_ref[...]


def _ret(pr, cos, sin, dec, qd, kd, cd, bd, gn_g, gmat, r0, chunk, rows, seqs):
    nb, t, _ = pr.shape
    npair = C_HEADS // 2
    assert nb % seqs == 0
    c2 = lambda b, i: (0, 0)
    c3 = lambda b, i: (0, 0, 0)
    return pl.pallas_call(
        functools.partial(_ret_kernel, chunk=chunk, rows=rows, seqs=seqs),
        grid=(nb // seqs, t // rows),
        in_specs=[pl.BlockSpec((seqs, rows, SEG_RET), lambda b, i: (b, i, 0)),
                  pl.BlockSpec((rows, LANE), lambda b, i: (i, 0)),
                  pl.BlockSpec((rows, LANE), lambda b, i: (i, 0)),
                  pl.BlockSpec(dec.shape, c3),
                  pl.BlockSpec(qd.shape, c2),
                  pl.BlockSpec(kd.shape, c2),
                  pl.BlockSpec(cd.shape, c3),
                  pl.BlockSpec(bd.shape, c2),
                  pl.BlockSpec((1, C_WIDTH), c2),
                  pl.BlockSpec(gmat.shape, c2),
                  pl.BlockSpec((seqs, npair, LANE, LANE), lambda b, i: (b, 0, 0, 0))],
        out_specs=[pl.BlockSpec((seqs, rows, C_WIDTH), lambda b, i: (b, i, 0)),
                   pl.BlockSpec((seqs, npair, LANE, LANE), lambda b, i: (b, 0, 0, 0))],
        out_shape=[jax.ShapeDtypeStruct((nb, t, C_WIDTH), F32),
                   jax.ShapeDtypeStruct((nb, npair, LANE, LANE), F32)],
        scratch_shapes=[pltpu.VMEM((seqs, npair, LANE, LANE), F32)],
        compiler_params=_params(2),
        name="ret",
    )(pr, cos, sin, dec, qd, kd, cd, bd, gn_g, gmat, r0)


def _lane_blocks(x):
    return [x[:, i * LANE:(i + 1) * LANE] for i in range(x.shape[1] // LANE)]


def _key_to_float(key):
    return pltpu.bitcast(key ^ ((key >> 31) & 0x7FFFFFFF), F32)


def _threshold_float(key):
    subnormal = jnp.logical_and(key > 0, key < MIN_NORMAL_KEY)
    return _key_to_float(jnp.where(subnormal, MIN_NORMAL_KEY, key))


def _float_to_key(x):
    bits = pltpu.bitcast(x, I32)
    return bits ^ ((bits >> 31) & 0x7FFFFFFF)


def _dsa_kernel(qb_ref, qi_ref, kw_ref, kv_ref, kx_ref, tri_ref, o_ref,
                sc_ref, wb_ref, m_ref, l_ref, acc_ref, mx_ref, mn_ref,
                *, groups, tg, nsb_total, causal, k_sel, n_keys):
    ks = KEY_SUPER
    nrep = ks // LANE
    tq = groups * tg
    if causal:
        j = pl.program_id(1)
        nsb = (j * tq) // ks + 1
        row = lax.broadcasted_iota(I32, (tq, 1), 0)
        limit = ((j * tq + row) // CHUNK + 1) * CHUNK
    else:
        nsb = nsb_total
        limit = jnp.full((tq, 1), n_keys, I32)
    lane = lax.broadcasted_iota(I32, (tq, ks), 1)

    def wide(x):
        return jnp.concatenate([x] * nrep, axis=1)

    def stack_heads(ref, g, n_heads):
        return jnp.concatenate([ref[g][:, h * LANE:(h + 1) * LANE] for h in range(n_heads)], axis=0)

    def group_rows(per_group):
        return jnp.concatenate(per_group, axis=0) if groups > 1 else per_group[0]

    qs = [stack_heads(qb_ref, g, B_HEADS) for g in range(groups)]
    qis = [stack_heads(qi_ref, g, IDX_HEADS) for g in range(groups)]

    for g in range(groups):
        w = kw_ref[g][:, IDX_DIM:IDX_DIM + IDX_HEADS] * (IDX_DIM ** -0.5 * IDX_HEADS ** -0.5)
        for h in range(IDX_HEADS):
            wb_ref[h, g * tg:(g + 1) * tg, :] = jnp.broadcast_to(w[:, h:h + 1], (tg, LANE))

    mx_ref[...] = jnp.full(mx_ref.shape, -jnp.inf, F32)
    mn_ref[...] = jnp.full(mn_ref.shape, jnp.inf, F32)

    def score_body(sb, carry):
        off = pl.multiple_of(sb * ks, ks)
        accs = []
        for g in range(groups):
            d = _dot_nt(qis[g], kx_ref[g, pl.ds(off, ks), :])
            acc = jnp.zeros((tg, ks), F32)
            for h in range(IDX_HEADS):
                acc = acc + jnp.maximum(d[h * tg:(h + 1) * tg], 0.0) * wide(wb_ref[h, g * tg:(g + 1) * tg, :])
            accs.append(acc)
        acc = group_rows(accs)
        adm = off + lane < limit
        sc = jnp.where(adm, acc, -jnp.inf)
        sc_ref[sb] = sc
        mx_ref[...] = functools.reduce(jnp.maximum, _lane_blocks(sc), mx_ref[...])
        mn_ref[...] = functools.reduce(jnp.minimum, _lane_blocks(jnp.where(adm, acc, jnp.inf)), mn_ref[...])
        k_sq = []
        for g in range(groups):
            kf = jnp.where(key_lanes, kv_ref[g, pl.ds(off, ks), :].astype(F32), 0.0)
            k_sq.append(jnp.maximum(carry[g], jnp.max(jnp.sum(kf * kf, axis=1, keepdims=True), axis=0, keepdims=True)))
        return tuple(k_sq)

    key_lanes = lax.broadcasted_iota(I32, (ks, LANE), 1) < HEAD_DIM
    k_sq = lax.fori_loop(0, nsb, score_body, tuple(jnp.zeros((1, 1), F32) for _ in range(groups)))

    rows = min(tq, LANE)
    dense = tq % LANE == 0

    def to_state(per_row_lanes, reduce_fn):
        if dense:
            return reduce_fn(per_row_lanes.T, axis=0, keepdims=True)
        return reduce_fn(per_row_lanes, axis=1, keepdims=True)

    def to_rows(state, r0):
        if dense:
            return jnp.broadcast_to(state[:, r0:r0 + rows], (rows, LANE)).T
        return jnp.broadcast_to(state, (rows, LANE))

    def gather_state(parts):
        return jnp.concatenate(parts, axis=1 if dense else 0) if len(parts) > 1 else parts[0]

    def count(cand, strict, live_groups=None):
        starts = range(0, tq, rows)
        cbs = [to_rows(cand, r0) for r0 in starts]
        accs = []
        for n, (r0, cb) in enumerate(zip(starts, cbs)):
            def body(sb, acc, r0=r0, cb=cb):
                for i in range(nrep):
                    x = sc_ref[sb, r0:r0 + rows, i * LANE:(i + 1) * LANE]
                    acc = acc + jnp.where(x > cb if strict else x >= cb, 1.0, 0.0)
                return acc

            def sweep(body=body):
                return lax.fori_loop(0, nsb, body, jnp.zeros((rows, LANE), F32))

            if live_groups is None:
                accs.append(sweep())
            else:
                accs.append(lax.cond(live_groups[n] > 0.5, sweep, lambda: jnp.zeros((rows, LANE), F32)))
        return gather_state([to_state(acc, jnp.sum) for acc in accs])

    if causal and dense:
        pos = j * tq + lax.broadcasted_iota(I32, (1, tq), 1)
        n_adm = ((pos // CHUNK + 1) * CHUNK).astype(F32)
    else:
        n_adm = jnp.minimum(limit, n_keys).astype(F32)
    row_min = gather_state([to_state(mn_ref[r0:r0 + rows], jnp.min) for r0 in range(0, tq, rows)])
    row_max = gather_state([to_state(mx_ref[r0:r0 + rows], jnp.max) for r0 in range(0, tq, rows)])
    short = n_adm <= k_sel
    lo0 = jnp.where(short, NEG_INF_KEY + 1, _float_to_key(row_min))
    hi0 = jnp.where(short, NEG_INF_KEY + 2, _float_to_key(row_max) + 1)
    c0 = jnp.where(short, float(k_sel), n_adm)

    def unresolved(lo, hi, c_lo):
        return jnp.logical_and(hi > lo + 1, c_lo != k_sel)

    def unresolved_groups(lo, hi, c_lo):
        u = jnp.where(unresolved(lo, hi, c_lo), 1.0, 0.0)
        if dense:
            return [jnp.max(u[:, r0:r0 + rows]) for r0 in range(0, tq, rows)]
        return [jnp.max(u)]

    def value_mid(lo, hi):
        return _float_to_key(0.5 * _key_to_float(lo) + 0.5 * _key_to_float(hi))

    def step(lo, hi, c_lo, cand, strict, live_groups=None):
        live = unresolved(lo, hi, c_lo)
        cand = jnp.minimum(jnp.maximum(cand, lo + 1 - strict), hi - 1 - strict)
        cnt = count(_threshold_float(cand), bool(strict), live_groups)
        edge = cand + strict
        take = jnp.logical_and(live, cnt >= k_sel)
        drop = jnp.logical_and(live, cnt < k_sel)
        return jnp.where(take, edge, lo), jnp.where(drop, edge, hi), jnp.where(take, cnt, c_lo)

    lo, hi, c_lo = step(lo0, hi0, c0, jnp.where(jnp.logical_and(lo0 < 0, hi0 > 0), 0, value_mid(lo0, hi0)), 0)
    lo, hi, c_lo = step(lo, hi, c_lo, jnp.where(lo == 0, 0, value_mid(lo, hi)), 1)

    def sel_cond(carry):
        return jnp.logical_and(carry[0] < SELECT_MAX_PASSES, carry[1] > 0.5)

    def sel_body(carry):
        it, _, lo, hi, c_lo = carry
        live_groups = unresolved_groups(lo, hi, c_lo)
        active = functools.reduce(jnp.maximum, live_groups)
        kmid = (lo >> 1) + (hi >> 1) + (lo & hi & 1)
        lo, hi, c_lo = step(lo, hi, c_lo, jnp.where(it < SELECT_VALUE_PASSES, value_mid(lo, hi), kmid), 0,
                            live_groups)
        return it + 1, active, lo, hi, c_lo

    _, _, lo, _, c_lo = lax.while_loop(
        sel_cond, sel_body,
        (jnp.int32(0), functools.reduce(jnp.maximum, unresolved_groups(lo, hi, c_lo)), lo, hi, c_lo))
    thr = _threshold_float(lo)

    def rows_of(state):
        return jnp.concatenate([to_rows(state, r0) for r0 in range(0, tq, rows)], axis=0)

    thr_rows = wide(rows_of(thr))

    any_tied = jnp.max(jnp.where(c_lo > k_sel, 1.0, 0.0)) > 0.5

    @pl.when(any_tied)
    def _():
        need = rows_of(k_sel - count(thr, True))
        thr_block = thr_rows[:, :LANE]

        def body(sb, before):
            for i in range(nrep):
                sc = sc_ref[sb, :, i * LANE:(i + 1) * LANE]
                eq = sc == thr_block
                both = _dot(jnp.where(eq, 1.0, 0.0).astype(BF16), tri_ref[...])
                drop = jnp.logical_and(eq, both[:, :LANE] + before >= need)
                sc_ref[sb, :, i * LANE:(i + 1) * LANE] = jnp.where(drop, -jnp.inf, sc)
                before = before + both[:, LANE:]
            return before

        lax.fori_loop(0, nsb, body, jnp.zeros((tq, LANE), F32))

    def selection_bias(sb):
        return jnp.where(sc_ref[sb] >= thr_rows, 0.0, -jnp.inf)

    def logits(sb):
        off = pl.multiple_of(sb * ks, ks)
        kvs = [kv_ref[g, pl.ds(off, ks), :] for g in range(groups)]
        s = [_dot_nt(qs[g], kvs[g]) for g in range(groups)]
        return [group_rows([s[g][h * tg:(h + 1) * tg] for g in range(groups)]) for h in range(B_HEADS)], kvs

    worst = jnp.zeros((1, 1), F32)
    for g in range(groups):
        qf = qs[g].astype(F32)
        bound = jnp.sqrt(jnp.sum(qf * qf, axis=1, keepdims=True) * k_sq[g]) * NORM_BOUND_SLACK
        worst = jnp.maximum(worst, jnp.max(bound, axis=0, keepdims=True))
        for h in range(B_HEADS):
            m_ref[h, g * tg:(g + 1) * tg, :] = jnp.broadcast_to(bound[h * tg:(h + 1) * tg], (tg, LANE))

    @pl.when(jnp.max(worst) > SAFE_LOGIT_BOUND)
    def _():
        m_ref[...] = jnp.full(m_ref.shape, -jnp.inf, F32)

        def max_body(sb, carry):
            s, _ = logits(sb)
            bias = selection_bias(sb)
            for h in range(B_HEADS):
                m_ref[h] = functools.reduce(jnp.maximum, _lane_blocks(s[h] + bias), m_ref[h])
            return carry

        lax.fori_loop(0, nsb, max_body, 0)
        for h in range(B_HEADS):
            m_ref[h] = jnp.broadcast_to(jnp.max(m_ref[h], axis=1, keepdims=True), (tq, LANE))

    l_ref[...] = jnp.zeros(l_ref.shape, F32)
    acc_ref[...] = jnp.zeros(acc_ref.shape, F32)

    def pv_body(sb, carry):
        s, kvs = logits(sb)
        bias = selection_bias(sb)
        ps = []
        for h in range(B_HEADS):
            p = jnp.exp(s[h] + bias - wide(m_ref[h]))
            l_ref[h] += sum(_lane_blocks(p))
            ps.append(p.astype(BF16))
        for g in range(groups):
            pg = jnp.concatenate([ps[h][g * tg:(g + 1) * tg] for h in range(B_HEADS)], axis=0)
            acc_ref[g * B_HEADS * tg:(g + 1) * B_HEADS * tg, :] += _dot(pg, kvs[g])
        return carry

    lax.fori_loop(0, nsb, pv_body, 0)

    lane_head = lax.broadcasted_iota(I32, (tg, LANE), 1)
    for g in range(groups):
        outs = [acc_ref[(g * B_HEADS + h) * tg:(g * B_HEADS + h + 1) * tg, :]
                / jnp.sum(l_ref[h, g * tg:(g + 1) * tg, :], axis=1, keepdims=True) for h in range(B_HEADS)]
        for p in range(B_HEADS // 2):
            even = pltpu.roll(outs[2 * p], HEAD_DIM, axis=1)
            o_ref[g, :, p * LANE:(p + 1) * LANE] = jnp.where(lane_head < HEAD_DIM, even, outs[2 * p + 1])


def _dsa(qb, qi, kw, kvb, kxb, tri, groups, tg, causal, n_keys):
    nb, t, _ = qb.shape
    nk = kvb.shape[1]
    nsb_total = nk // KEY_SUPER
    tq = groups * tg
    assert nb % groups == 0 and t % tg == 0 and not (causal and groups > 1)
    kernel = functools.partial(_dsa_kernel, groups=groups, tg=tg, nsb_total=nsb_total, causal=causal,
                               k_sel=min(TOPK_MAX, n_keys // 4), n_keys=n_keys)
    return pl.pallas_call(
        kernel,
        grid=(nb // groups, t // tg),
        in_specs=[pl.BlockSpec((groups, tg, SEG_QB), lambda b, j: (b, j, 0)),
                  pl.BlockSpec((groups, tg, SEG_QI), lambda b, j: (b, j, 0)),
                  pl.BlockSpec((groups, tg, SEG_KW), lambda b, j: (b, j, 0)),
                  pl.BlockSpec((groups, nk, SEG_KV), lambda b, j: (b, 0, 0)),
                  pl.BlockSpec((groups, nk, SEG_KW), lambda b, j: (b, 0, 0)),
                  pl.BlockSpec(tri.shape, lambda b, j: (0, 0))],
        out_specs=pl.BlockSpec((groups, tg, B_WIDTH), lambda b, j: (b, j, 0)),
        out_shape=jax.ShapeDtypeStruct((nb, t, B_WIDTH), F32),
        scratch_shapes=[pltpu.VMEM((nsb_total, tq, KEY_SUPER), F32),
                        pltpu.VMEM((IDX_HEADS, tq, LANE), F32),
                        pltpu.VMEM((B_HEADS, tq, LANE), F32),
                        pltpu.VMEM((B_HEADS, tq, LANE), F32),
                        pltpu.VMEM((B_HEADS * tq, LANE), F32),
                        pltpu.VMEM((tq, LANE), F32),
                        pltpu.VMEM((tq, LANE), F32)],
        compiler_params=_params(2),
        name="dsa",
    )(qb, qi, kw, kvb, kxb, tri)


def _ffn_kernel(*refs, tm, alpha, streaming, seq):
    if streaming:
        (x_ref, oa_ref, ob_ref, oc_ref, wo_ref, g1_ref, b1_ref, wg_ref, wu_ref, cw_ref, cb_ref, wd_ref,
         g2_ref, b2_ref, y_ref, cs_ref, x1_ref, x1b_ref, acc_ref, carry_ref) = refs
    else:
        (x_ref, oa_ref, ob_ref, oc_ref, wo_ref, g1_ref, b1_ref, wg_ref, wu_ref, cw_ref, cb_ref, wd_ref,
         g2_ref, b2_ref, p1_ref, p2_ref, y_ref, cs_ref, x1_ref, x1b_ref, acc_ref) = refs
    f = pl.program_id(1)
    nf = pl.num_programs(1)

    @pl.when(f == 0)
    def _():
        mix = (_dot(oa_ref[...].astype(BF16), wo_ref[0:A_WIDTH, :])
               + _dot(ob_ref[...].astype(BF16), wo_ref[A_WIDTH:A_WIDTH + B_WIDTH, :])
               + _dot(oc_ref[...].astype(BF16), wo_ref[A_WIDTH + B_WIDTH:, :]))
        x1 = _layer_norm(alpha * x_ref[...] + mix, g1_ref[...], b1_ref[...])
        x1_ref[...] = x1
        x1b_ref[...] = x1.astype(BF16)
        acc_ref[...] = jnp.zeros_like(acc_ref)

    xb = x1b_ref[...]
    hg = _dot(xb, wg_ref[...])
    hu = _dot(xb, wu_ref[...])
    tf = hg.shape[1]
    h1 = pltpu.roll(hg, 1, axis=0)
    h2 = pltpu.roll(hg, 2, axis=0)
    if streaming:
        @pl.when((pl.program_id(0) * tm) % seq == 0)
        def _():
            carry_ref[f] = jnp.zeros((8, tf), F32)

        prev = carry_ref[f]
        c0 = prev[6:7, :]
        c1 = prev[7:8, :]
        top = lax.broadcasted_iota(I32, (8, tf), 0)
        h1 = jnp.concatenate([jnp.where(top == 0, c1, h1[:8]), h1[8:]], axis=0)
        h2 = jnp.concatenate([jnp.where(top == 0, c0, jnp.where(top == 1, c1, h2[:8])), h2[8:]], axis=0)
        carry_ref[f] = hg[tm - 8:, :]
        cs_ref[0] = hg[tm - 8:, :]
    else:
        pos = lax.broadcasted_iota(I32, (tm, tf), 0) % seq
        h1 = jnp.where(pos == 0, p1_ref[...], h1)
        h2 = jnp.where(pos < 2, p2_ref[...], h2)
        cs_ref[...] = hg
    cw = cw_ref[...]
    conv = cb_ref[...] + cw[0:1, :] * h2 + cw[1:2, :] * h1 + cw[2:3, :] * hg
    act = jax.nn.gelu(conv) * hu
    acc_ref[...] += _dot(act.astype(BF16), wd_ref[...])

    @pl.when(f == nf - 1)
    def _():
        y_ref[...] = _layer_norm(alpha * x1_ref[...] + acc_ref[...], g2_ref[...], b2_ref[...])


def _ffn(x2d, oa, ob, oc, lw, tm, tf, seq, prev=None):
    t = x2d.shape[0]
    nf = D_FF // tf
    streaming = prev is None
    alpha = lw["alpha"]
    row = lambda i, f: (i, 0)
    const = lambda i, f: (0, 0)
    in_specs = [pl.BlockSpec((tm, D_MODEL), row),
                pl.BlockSpec((tm, A_WIDTH), row),
                pl.BlockSpec((tm, B_WIDTH), row),
                pl.BlockSpec((tm, C_WIDTH), row),
                pl.BlockSpec((D_MODEL, D_MODEL), const),
                pl.BlockSpec((1, D_MODEL), const),
                pl.BlockSpec((1, D_MODEL), const),
                pl.BlockSpec((D_MODEL, tf), lambda i, f: (0, f)),
                pl.BlockSpec((D_MODEL, tf), lambda i, f: (0, f)),
                pl.BlockSpec((CONV_W, tf), lambda i, f: (0, f)),
                pl.BlockSpec((1, tf), lambda i, f: (0, f)),
                pl.BlockSpec((tf, D_MODEL), lambda i, f: (f, 0)),
                pl.BlockSpec((1, D_MODEL), const),
                pl.BlockSpec((1, D_MODEL), const)]
    args = [x2d, oa, ob, oc, lw["w_out"], lw["ln1_g"], lw["ln1_b"], lw["w_gate"], lw["w_up"],
            lw["conv_w"], lw["conv_b"], lw["w_down"], lw["ln2_g"], lw["ln2_b"]]
    scratch = [pltpu.VMEM((tm, D_MODEL), F32), pltpu.VMEM((tm, D_MODEL), BF16),
               pltpu.VMEM((tm, D_MODEL), F32)]
    if streaming:
        cs_spec = pl.BlockSpec((1, 8, tf), lambda i, f: (i, 0, f))
        cs_shape = jax.ShapeDtypeStruct((t // tm, 8, D_FF), F32)
        scratch.append(pltpu.VMEM((nf, 8, tf), F32))
    else:
        in_specs += [pl.BlockSpec((tm, tf), lambda i, f: (i, f))] * 2
        args += list(prev)
        cs_spec = pl.BlockSpec((tm, tf), lambda i, f: (i, f))
        cs_shape = jax.ShapeDtypeStruct((t, D_FF), F32)
    return pl.pallas_call(
        functools.partial(_ffn_kernel, tm=tm, alpha=alpha, streaming=streaming, seq=seq),
        grid=(t // tm, nf),
        in_specs=in_specs,
        out_specs=[pl.BlockSpec((tm, D_MODEL), row), cs_spec],
        out_shape=[jax.ShapeDtypeStruct((t, D_MODEL), F32), cs_shape],
        scratch_shapes=scratch,
        compiler_params=_params(2, FFN_VMEM_LIMIT),
        name="ffn",
    )(*args)


def _pad_cols(w, width):
    return jnp.pad(w, ((0, 0), (0, width - w.shape[1])))


def _proj_weight(w_in):
    sizes = (A_WIDTH, A_WIDTH, B_WIDTH, HEAD_DIM, HEAD_DIM, IDX_HEADS * IDX_DIM, IDX_DIM, IDX_HEADS,
             C_WIDTH, C_WIDTH, C_WIDTH, C_WIDTH)
    splits = np.cumsum(sizes)[:-1].tolist()
    ua, va, qb, kb, vb, qib, kib, wib, qc, kc, vc, gc = jnp.split(w_in, splits, axis=1)
    qb = qb * (HEAD_DIM ** -0.5)
    cols = [ua, va]
    cols += [_pad_cols(qb[:, h * HEAD_DIM:(h + 1) * HEAD_DIM], LANE) for h in range(B_HEADS)]
    cols += [qc, kc, vc, gc]
    cols += [_pad_cols(qib[:, h * IDX_DIM:(h + 1) * IDX_DIM], LANE) for h in range(IDX_HEADS)]
    cols += [kb, vb, _pad_cols(jnp.concatenate([kib, wib], axis=1), LANE)]
    return jnp.concatenate(cols, axis=1).astype(BF16)


def _group_matrix(width):
    g = np.arange(width) // HEAD_DIM
    return jnp.asarray((g[:, None] == g[None, :]).astype(np.float32) / HEAD_DIM, BF16)


def _rotary_tables(pos):
    half = HEAD_DIM // 2
    freqs = ROPE_BASE ** (-jnp.arange(half, dtype=F32) / half)
    ang = pos.astype(F32)[:, None] * freqs
    cos, sin = jnp.cos(ang), jnp.sin(ang)
    cos = jnp.concatenate([cos, cos, cos, cos], axis=1)
    sin = jnp.concatenate([-sin, sin, -sin, sin], axis=1)
    return cos, sin


def _retention_tables(c):
    log_g = jnp.log(1.0 - 2.0 ** (-5.0 - jnp.arange(C_HEADS, dtype=F32)))
    i = jnp.arange(c, dtype=F32)
    diff = i[:, None] - i[None, :]
    dec = jnp.where(diff >= 0, jnp.exp(jnp.maximum(diff, 0.0)[None] * log_g[:, None, None]), 0.0)
    qd = jnp.repeat(jnp.exp((i + 1)[None] * log_g[:, None]).T, HEAD_DIM, axis=1)
    kd = jnp.repeat(jnp.exp((c - 1 - i)[None] * log_g[:, None]).T, HEAD_DIM, axis=1)
    cdl = jnp.repeat(jnp.exp(c * log_g), HEAD_DIM).reshape(C_HEADS // 2, LANE)
    cd = jnp.broadcast_to(cdl[:, :, None], (C_HEADS // 2, LANE, LANE))
    blk = np.arange(LANE) // HEAD_DIM
    bd = jnp.asarray((blk[:, None] == blk[None, :]).astype(np.float32))
    return dec, qd, kd, cd, bd


def _pair_states(r):
    n = r.shape[0]
    r = r.reshape(n, C_HEADS // 2, 2, HEAD_DIM, HEAD_DIM).astype(F32)
    z = jnp.zeros_like(r[:, :, 0])
    top = jnp.concatenate([r[:, :, 0], z], axis=-1)
    bot = jnp.concatenate([z, r[:, :, 1]], axis=-1)
    return jnp.concatenate([top, bot], axis=-2)


def _unpair_states(s):
    a = s[:, :, :HEAD_DIM, :HEAD_DIM]
    b = s[:, :, HEAD_DIM:, HEAD_DIM:]
    n = s.shape[0]
    return jnp.stack([a, b], axis=2).reshape(n, C_HEADS, HEAD_DIM, HEAD_DIM)


def _mixa_tables(ws, bs, c):
    tril = jnp.tril(jnp.ones((c, c), ws.dtype))
    w = ws[:, :c, :c] * tril
    wcat = jnp.concatenate([w[g] for g in range(A_GROUPS)], axis=1).astype(BF16)
    bias = jnp.repeat(bs[:, :c].T, HEAD_DIM, axis=1)
    return wcat, bias


def _layer_weights(l, depth, w_out, ln1_g, ln1_b, w_gate, w_up, conv_w, conv_b, w_down, ln2_g, ln2_b):
    return dict(alpha=float((2 * depth) ** 0.25),
                w_out=w_out[l].astype(BF16), ln1_g=ln1_g[l][None], ln1_b=ln1_b[l][None],
                w_gate=w_gate[l].astype(BF16), w_up=w_up[l].astype(BF16), conv_w=conv_w[l],
                conv_b=conv_b[l][None], w_down=w_down[l].astype(BF16),
                ln2_g=ln2_g[l][None], ln2_b=ln2_b[l][None])


def kernel(x_prompt, x_sample, cache_b_k, cache_b_v, cache_b_kidx, state_ret, state_ffn_conv,
           w_in, a_ln_g, a_ln_b, a_ws, a_bs, c_gn_g, w_out, ln1_g, ln1_b,
           w_gate, w_up, conv_w, conv_b, w_down, ln2_g, ln2_b):
    depth = w_in.shape[0]
    bp, s, _ = x_prompt.shape
    bs_, t, _ = x_sample.shape
    past = cache_b_k.shape[2]
    l_keys = past + t
    ts = bs_ * t
    tm_p = 512
    tm_s = ts if ts <= 512 else 512
    tf_p, tf_s = D_FF // 2, 256
    assert s % tm_p == 0 and s % A_CHUNK == 0 and ts % tm_s == 0 and tm_s % t == 0 and t >= 2
    assert s % KEY_SUPER == 0 and KEY_SUPER % QUERY_BLOCK == 0
    nk_s = -(-l_keys // KEY_SUPER) * KEY_SUPER
    groups_s = max(g for g in range(1, max(LANE // t, 1) + 1) if bs_ % g == 0)

    gmat_a = _group_matrix(A_WIDTH)
    gmat_c = _group_matrix(C_WIDTH)
    tri = jnp.asarray(np.concatenate([np.triu(np.ones((LANE, LANE), np.float32), 1),
                                      np.ones((LANE, LANE), np.float32)], axis=1), BF16)
    cos_p, sin_p = _rotary_tables(jnp.arange(s))
    cos_s, sin_s = _rotary_tables(past + jnp.arange(t))
    ret_p = _retention_tables(CHUNK)
    ret_s = _retention_tables(t)

    xp = x_prompt.reshape(bp * s, D_MODEL)
    xs = x_sample.reshape(ts, D_MODEL)
    outs = {k: [] for k in ("kp", "vp", "kip", "rp", "cp", "ks", "vs", "kis", "rs", "cs", "avs")}
    for l in range(depth):
        w_proj = _proj_weight(w_in[l])
        lw = _layer_weights(l, depth, w_out, ln1_g, ln1_b, w_gate, w_up, conv_w, conv_b, w_down, ln2_g, ln2_b)
        ln_g, ln_b, gn_g = a_ln_g[l][None], a_ln_b[l][None], c_gn_g[l][None]

        pa, qb, pr, qi, kv, kw, kvb, kwb = _proj(xp, w_proj, tm_p)
        wcat, bias = _mixa_tables(a_ws[l], a_bs[l], A_CHUNK)
        oa, _ = _mixa(pa, wcat, bias, ln_g, ln_b, gmat_a, A_CHUNK, 512)
        r0 = jnp.zeros((bp, C_HEADS // 2, LANE, LANE), F32)
        oc, r_p = _ret(pr.reshape(bp, s, SEG_RET), cos_p, sin_p, *ret_p, gn_g, gmat_c, r0, CHUNK, 512,
                          2 if bp % 2 == 0 else 1)
        ob = _dsa(qb.reshape(bp, s, SEG_QB), qi.reshape(bp, s, SEG_QI), kw.reshape(bp, s, SEG_KW),
                  kvb.reshape(bp, s, SEG_KV), kwb.reshape(bp, s, SEG_KW), tri, 1, QUERY_BLOCK, True, s)
        xp, c_p = _ffn(xp, oa, ob.reshape(bp * s, B_WIDTH), oc.reshape(bp * s, C_WIDTH), lw, tm_p, tf_p, s)
        kv3 = kv.reshape(bp, s, SEG_KV)
        outs["kp"].append(kv3[..., :HEAD_DIM])
        outs["vp"].append(kv3[..., HEAD_DIM:])
        outs["kip"].append(kw.reshape(bp, s, SEG_KW)[..., :IDX_DIM])
        outs["rp"].append(_unpair_states(r_p))
        outs["cp"].append(c_p[s // tm_p - 1::s // tm_p, 8 - (CONV_W - 1):, :])

        pa, qb, pr, qi, kv, kw, kvb, kwb = _proj(xs, w_proj, tm_s)
        wcat, bias = _mixa_tables(a_ws[l], a_bs[l], t)
        oa, av = _mixa(pa, wcat, bias, ln_g, ln_b, gmat_a, t, tm_s)
        r0 = _pair_states(state_ret[l])
        oc, r_s = _ret(pr.reshape(bs_, t, SEG_RET), cos_s, sin_s, *ret_s, gn_g, gmat_c, r0, t, t, groups_s)
        pad = nk_s - l_keys
        kv_full = jnp.concatenate(
            [jnp.concatenate([cache_b_k[l], cache_b_v[l]], axis=-1).astype(BF16),
             kvb.reshape(bs_, t, SEG_KV), jnp.zeros((bs_, pad, SEG_KV), BF16)], axis=1)
        kx_full = jnp.concatenate(
            [jnp.pad(cache_b_kidx[l], ((0, 0), (0, 0), (0, SEG_KW - IDX_DIM))).astype(BF16),
             kwb.reshape(bs_, t, SEG_KW), jnp.zeros((bs_, pad, SEG_KW), BF16)], axis=1)
        ob = _dsa(qb.reshape(bs_, t, SEG_QB), qi.reshape(bs_, t, SEG_QI), kw.reshape(bs_, t, SEG_KW),
                  kv_full, kx_full, tri, groups_s, t, False, l_keys)
        cprev = state_ffn_conv[l]
        zeros = jnp.zeros((bs_, t - 2, D_FF), F32)
        p1 = jnp.concatenate([cprev[:, 1:2], zeros, zeros[:, :1]], axis=1).reshape(ts, D_FF)
        p2 = jnp.concatenate([cprev, zeros], axis=1).reshape(ts, D_FF)
        xs, hg_s = _ffn(xs, oa, ob.reshape(ts, B_WIDTH), oc.reshape(ts, C_WIDTH), lw, tm_s, tf_s, t, (p1, p2))
        kv3 = kv.reshape(bs_, t, SEG_KV)
        outs["ks"].append(kv3[..., :HEAD_DIM])
        outs["vs"].append(kv3[..., HEAD_DIM:])
        outs["kis"].append(kw.reshape(bs_, t, SEG_KW)[..., :IDX_DIM])
        outs["rs"].append(_unpair_states(r_s))
        outs["cs"].append(hg_s.reshape(bs_, t, D_FF)[:, t - (CONV_W - 1):, :])
        outs["avs"].append(av.reshape(bs_, t, A_WIDTH))

    st = lambda k: jnp.stack(outs[k])
    return (xp.reshape(bp, s, D_MODEL), xs.reshape(bs_, t, D_MODEL),
            st("kp"), st("vp"), st("kip"), st("rp"), st("cp"),
            st("ks"), st("vs"), st("kis"), st("rs"), st("cs"), st("avs"))
```
